```python
import jax, jax.numpy as jnp
from jax import lax
import numpy as np

D_MODEL = 2048
BATCH = 4
SEQ = 4096
DEPTH = 1

A_HEADS = 8
A_HEAD_DIM = 128
A_WIDTH = A_HEADS * A_HEAD_DIM
MOBA_BLOCK = 256
MOBA_TOPK = 3
Q_CHUNK = 16
R_HEADS = 8
R_KEY_DIM = 128
R_VAL_DIM = 128
R_WIDTH = R_HEADS * R_KEY_DIM
R_CHUNK = 64
N_GROUPS = 4
EXPERTS_PER_GROUP = 8
N_EXPERTS = N_GROUPS * EXPERTS_PER_GROUP
TOP_K_INNER = 2
D_EXPERT = 512
MOE_BLOCK = 256
EPS = 1e-6
IN_SPLITS = (A_WIDTH, A_WIDTH, A_WIDTH, R_WIDTH, R_WIDTH, R_WIDTH, R_WIDTH, D_MODEL, D_MODEL)
IN_COLS = sum(IN_SPLITS)

kernel_name = "hybrid_moba_hgrn2_hiermoe_adaln"


def rms_norm(x, g):
    xf = x.astype(jnp.float32)
    y = xf * lax.rsqrt(jnp.mean(xf * xf, axis=-1, keepdims=True) + EPS)
    return (y * g.astype(jnp.float32)).astype(x.dtype)


def split_heads(t, n_heads):
    b, s, w = t.shape
    return t.reshape(b, s, n_heads, w // n_heads).transpose(0, 2, 1, 3)


def merge_heads(t):
    b, h, s, d = t.shape
    return t.transpose(0, 2, 1, 3).reshape(b, s, h * d)


def moba_attention(q, k, v):
    B, H, S, Dh = q.shape
    L = MOBA_BLOCK
    s_pad = -(-S // L) * L
    pad = ((0, 0), (0, 0), (0, s_pad - S), (0, 0))
    q, k, v = jnp.pad(q, pad), jnp.pad(k, pad), jnp.pad(v, pad)
    nb = s_pad // L
    kb = k.reshape(B, H, nb, L, Dh)
    vb = v.reshape(B, H, nb, L, Dh)
    k_mean = jnp.mean(kb.astype(jnp.float32), axis=3)
    gate = jnp.einsum('bhsd,bhnd->bhsn', q.astype(jnp.float32), k_mean)
    q_blk = jnp.arange(s_pad) // L
    past = jnp.arange(nb)[None, :] < q_blk[:, None]
    gate = jnp.where(past, gate, -jnp.inf)
    if nb < MOBA_TOPK:
        gate = jnp.pad(gate, ((0, 0), (0, 0), (0, 0), (0, MOBA_TOPK - nb)), constant_values=-jnp.inf)
    _, sel = lax.top_k(gate, MOBA_TOPK)
    valid = sel < q_blk[:, None]
    sel = jnp.minimum(sel, nb - 1)
    nc = s_pad // Q_CHUNK
    scale = Dh ** -0.5

    def chunks(t):
        return jnp.moveaxis(t.reshape(B, H, nc, Q_CHUNK, t.shape[-1]), 2, 0)

    gather = jax.vmap(jax.vmap(lambda blocks, idx: blocks[idx]))

    def one_chunk(args):
        qc, selc, validc, ci = args
        kg = gather(kb, selc)
        vg = gather(vb, selc)
        s_sel = jnp.einsum('bhqd,bhqkld->bhqkl', qc, kg, preferred_element_type=jnp.float32) * scale
        s_sel = jnp.where(validc[..., None], s_sel, -jnp.inf)
        q0 = ci * Q_CHUNK
        blk = q0 // L
        k_own = lax.dynamic_slice_in_dim(k, blk * L, L, axis=2)
        v_own = lax.dynamic_slice_in_dim(v, blk * L, L, axis=2)
        q_pos = q0 + jnp.arange(Q_CHUNK)
        k_pos = blk * L + jnp.arange(L)
        s_own = jnp.einsum('bhqd,bhld->bhql', qc, k_own, preferred_element_type=jnp.float32) * scale
        s_own = jnp.where(k_pos[None, :] <= q_pos[:, None], s_own, -jnp.inf)
        s_all = jnp.concatenate([s_sel.reshape(B, H, Q_CHUNK, MOBA_TOPK * L), s_own], axis=-1)
        p = jax.nn.softmax(s_all, axis=-1).astype(v.dtype)
        p_sel = p[..., :MOBA_TOPK * L].reshape(B, H, Q_CHUNK, MOBA_TOPK, L)
        p_own = p[..., MOBA_TOPK * L:]
        return (jnp.einsum('bhqkl,bhqkld->bhqd', p_sel, vg)
                + jnp.einsum('bhql,bhld->bhqd', p_own, v_own))

    out = lax.map(one_chunk, (chunks(q), chunks(sel), chunks(valid), jnp.arange(nc)))
    out = jnp.moveaxis(out, 0, 2).reshape(B, H, s_pad, Dh)
    return out[:, :, :S]


def hgrn2_recurrence(q, k, v, log_f):
    B, H, S, Dk = q.shape
    Dv = v.shape[-1]
    C = R_CHUNK
    nc = S // C

    def chunks(t):
        return jnp.moveaxis(t.astype(jnp.float32).reshape(B, H, nc, C, t.shape[-1]), 2, 0)

    causal = jnp.tril(jnp.ones((C, C), dtype=bool))

    def step(state, inp):
        qc, kc, vc, gc = inp
        b = jnp.cumsum(gc, axis=2)
        diff = b[:, :, :, None, :] - b[:, :, None, :, :]
        decay = jnp.exp(jnp.where(causal[:, :, None], diff, -jnp.inf))
        attn = jnp.einsum('bhtd,bhsd,bhtsd->bhts', qc, kc, decay)
        o = (jnp.einsum('bhts,bhse->bhte', attn, vc)
             + jnp.einsum('bhtd,bhde->bhte', qc * jnp.exp(b), state))
        b_last = b[:, :, -1:, :]
        state = (jnp.exp(b_last)[:, :, 0, :, None] * state
                 + jnp.einsum('bhsd,bhse->bhde', kc * jnp.exp(b_last - b), vc))
        return state, o

    state0 = jnp.zeros((B, H, Dk, Dv), jnp.float32)
    _, o = lax.scan(step, state0, (chunks(q), chunks(k), chunks(v), chunks(log_f)))
    return jnp.moveaxis(o, 0, 2).reshape(B, H, S, Dv)


def hybrid_mixer(h, w_in, lb, r_norm_g, w_up_a, w_up_r, w_out):
    proj = h @ w_in
    idx = np.cumsum(IN_SPLITS)[:-1].tolist()
    a_q, a_k, a_v, r_q, r_f, r_i, r_og, g_a, g_r = jnp.split(proj, idx, axis=-1)
    att = moba_attention(split_heads(a_q, A_HEADS), split_heads(a_k, A_HEADS), split_heads(a_v, A_HEADS))
    y_a = merge_heads(att.astype(h.dtype)) @ w_up_a
    lb_h = lb.reshape(1, R_HEADS, 1, R_KEY_DIM)
    f_logit = split_heads(r_f, R_HEADS).astype(jnp.float32)
    f = lb_h + (1.0 - lb_h) * jax.nn.sigmoid(f_logit)
    k_in = (1.0 - lb_h) * jax.nn.sigmoid(-f_logit)
    o = hgrn2_recurrence(split_heads(r_q, R_HEADS), k_in, split_heads(r_i, R_HEADS), jnp.log(f))
    o = rms_norm(o, r_norm_g).astype(h.dtype)
    o = merge_heads(o) * jax.nn.silu(r_og)
    y_r = o @ w_up_r
    merged = jax.nn.sigmoid(g_a) * y_a + jax.nn.sigmoid(g_r) * y_r
    return merged @ w_out


def hier_moe(h, w_rg, b_rg, w_re, b_re, w1, w3, w2):
    B, S, D = h.shape
    T = B * S
    ht = h.reshape(T, D)
    lg = (ht @ w_rg + b_rg).astype(jnp.float32)
    pg = jax.nn.softmax(lg, axis=-1)
    grp = jnp.argmax(lg, axis=-1)
    pg_top = jnp.take_along_axis(pg, grp[:, None], axis=-1)
    le = (ht @ w_re + b_re).astype(jnp.float32).reshape(T, N_GROUPS, EXPERTS_PER_GROUP)
    le_g = jnp.take_along_axis(le, grp[:, None, None], axis=1)[:, 0]
    pe = jax.nn.softmax(le_g, axis=-1)
    top_p, top_e = lax.top_k(pe, TOP_K_INNER)
    wts = pg_top * top_p / jnp.sum(top_p, axis=-1, keepdims=True)
    eid = grp[:, None] * EXPERTS_PER_GROUP + top_e
    TK = T * TOP_K_INNER
    flat_e = eid.reshape(TK)
    flat_tok = jnp.repeat(jnp.arange(T, dtype=jnp.int32), TOP_K_INNER)
    flat_w = wts.reshape(TK)
    order = jnp.argsort(flat_e)
    se = flat_e[order]
    counts = jnp.bincount(flat_e, length=N_EXPERTS)
    padded = (counts + MOE_BLOCK - 1) // MOE_BLOCK * MOE_BLOCK
    pad_end = jnp.cumsum(padded)
    pad_start = pad_end - padded
    start = jnp.cumsum(counts) - counts
    dest = pad_start[se] + (jnp.arange(TK) - start[se])
    R = TK + N_EXPERTS * MOE_BLOCK
    row_tok = jnp.zeros((R,), jnp.int32).at[dest].set(flat_tok[order])
    row_w = jnp.zeros((R,), jnp.float32).at[dest].set(flat_w[order])
    nblk = R // MOE_BLOCK
    blk_e = jnp.minimum(jnp.searchsorted(pad_end, jnp.arange(nblk) * MOE_BLOCK, side='right'), N_EXPERTS - 1)
    xr = ht[row_tok].reshape(nblk, MOE_BLOCK, D)

    def expert_block(args):
        xb, e = args
        hid = jax.nn.silu(xb @ w1[e]) * (xb @ w3[e])
        return hid @ w2[e]

    yr = lax.map(expert_block, (xr, blk_e)).reshape(R, D)
    y = jnp.zeros((T, D), h.dtype).at[row_tok].add(yr * row_w[:, None].astype(h.dtype))
    return y.reshape(B, S, D)


def setup_inputs(seed: int = 0) -> dict:
    key = jax.random.key(seed)
    ks = jax.random.split(key, 21)
    f32 = jnp.float32
    nrm = lambda k, shape, s: jax.random.normal(k, shape, f32) * s
    D = D_MODEL
    return {
        "x": nrm(ks[0], (BATCH, SEQ, D), 1.0),
        "c": nrm(ks[1], (BATCH, D), 1.0),
        "norm1_g": 1.0 + nrm(ks[2], (DEPTH, D), 0.02),
        "norm2_g": 1.0 + nrm(ks[3], (DEPTH, D), 0.02),
        "final_g": 1.0 + nrm(ks[4], (D,), 0.02),
        "w_ada": nrm(ks[5], (DEPTH, D, 6 * D), 0.5 * D ** -0.5),
        "b_ada": nrm(ks[6], (DEPTH, 6 * D), 0.02),
        "w_in": nrm(ks[7], (DEPTH, D, IN_COLS), D ** -0.5),
        "r_lower": nrm(ks[8], (DEPTH + 1, R_WIDTH), 0.5),
        "r_norm_g": 1.0 + nrm(ks[9], (DEPTH, R_VAL_DIM), 0.02),
        "w_up_a": nrm(ks[10], (DEPTH, A_WIDTH, D), A_WIDTH ** -0.5),
        "w_up_r": nrm(ks[11], (DEPTH, R_WIDTH, D), R_WIDTH ** -0.5),
        "w_out": nrm(ks[12], (DEPTH, D, D), D ** -0.5),
        "w_rg": nrm(ks[13], (DEPTH, D, N_GROUPS), D ** -0.5),
        "b_rg": nrm(ks[14], (DEPTH, N_GROUPS), 0.01),
        "w_re": nrm(ks[15], (DEPTH, D, N_EXPERTS), D ** -0.5),
        "b_re": nrm(ks[16], (DEPTH, N_EXPERTS), 0.01),
        "w1": nrm(ks[17], (DEPTH, N_EXPERTS, D, D_EXPERT), D ** -0.5),
        "w3": nrm(ks[18], (DEPTH, N_EXPERTS, D, D_EXPERT), D ** -0.5),
        "w2": nrm(ks[19], (DEPTH, N_EXPERTS, D_EXPERT, D), D_EXPERT ** -0.5),
    }


def reference(x, c, norm1_g, norm2_g, final_g, w_ada, b_ada, w_in, r_lower, r_norm_g,
              w_up_a, w_up_r, w_out, w_rg, b_rg, w_re, b_re, w1, w3, w2):
    lb_all = jnp.cumsum(jax.nn.softmax(r_lower.astype(jnp.float32), axis=0), axis=0)
    c_act = jax.nn.silu(c)
    for l in range(DEPTH):
        mod = (c_act @ w_ada[l] + b_ada[l])[:, None, :]
        sh1, sc1, gt1, sh2, sc2, gt2 = jnp.split(mod, 6, axis=-1)
        h = rms_norm(x, norm1_g[l]) * (1.0 + sc1) + sh1
        x = x + gt1 * hybrid_mixer(h, w_in[l], lb_all[l], r_norm_g[l], w_up_a[l], w_up_r[l], w_out[l])
        h = rms_norm(x, norm2_g[l]) * (1.0 + sc2) + sh2
        x = x + gt2 * hier_moe(h, w_rg[l], b_rg[l], w_re[l], b_re[l], w1[l], w3[l], w2[l])
    return rms_norm(x, final_g)
```

```python
import functools

import jax
import jax.numpy as jnp
from jax import lax
from jax.experimental import pallas as pl
from jax.experimental.pallas import tpu as pltpu

F32 = jnp.float32
BF16 = jnp.bfloat16
HIGHEST = lax.Precision.HIGHEST

D_MODEL = 2048
A_HEADS = 8
A_HEAD_DIM = 128
A_WIDTH = A_HEADS * A_HEAD_DIM
MOBA_BLOCK = 256
MOBA_TOPK = 3
R_HEADS = 8
R_KEY_DIM = 128
R_VAL_DIM = 128
R_WIDTH = R_HEADS * R_KEY_DIM
N_GROUPS = 4
EXPERTS_PER_GROUP = 8
N_EXPERTS = N_GROUPS * EXPERTS_PER_GROUP
TOP_K_INNER = 2
D_EXPERT = 512
MOE_BLOCK = 256
EPS = 1e-6
IN_COLS = 3 * A_WIDTH + 4 * R_WIDTH + 2 * D_MODEL

LANES = 128
CB_AQ = 0
CB_AK = CB_AQ + A_WIDTH // LANES
CB_AV = CB_AK + A_WIDTH // LANES
CB_RQ = CB_AV + A_WIDTH // LANES
CB_RF = CB_RQ + R_WIDTH // LANES
CB_RI = CB_RF + R_WIDTH // LANES
CB_ROG = CB_RI + R_WIDTH // LANES
CB_GA = CB_ROG + R_WIDTH // LANES
CB_GR = CB_GA + D_MODEL // LANES

HGRN_CHUNK = 128
HGRN_SUB = 16
NEG_BIG = -1e30
VMEM_LIMIT = 56 * 1024 * 1024


def _cparams(sem):
    return pltpu.CompilerParams(dimension_semantics=sem, vmem_limit_bytes=VMEM_LIMIT)


def _dot_nt(a, b, **kw):
    return lax.dot_general(a, b, (((1,), (1,)), ((), ())), preferred_element_type=F32, **kw)


def _dot_tn(a, b):
    return lax.dot_general(a, b, (((0,), (0,)), ((), ())), preferred_element_type=F32)


def _ada_kernel(c_ref, w_ref, b_ref, o_ref):
    c = c_ref[...]
    ca = c * jax.nn.sigmoid(c)
    o_ref[...] = jnp.dot(ca, w_ref[...], preferred_element_type=F32, precision=HIGHEST) + b_ref[...]


def _ada(c_pad, w_ada, b_ada):
    rows, d = c_pad.shape
    n = w_ada.shape[1]
    tn = 1024
    return pl.pallas_call(
        _ada_kernel,
        grid=(n // tn,),
        in_specs=[pl.BlockSpec((rows, d), lambda j: (0, 0)),
                  pl.BlockSpec((d, tn), lambda j: (0, j)),
                  pl.BlockSpec((1, tn), lambda j: (0, j))],
        out_specs=pl.BlockSpec((rows, tn), lambda j: (0, j)),
        out_shape=jax.ShapeDtypeStruct((rows, n), F32),
        compiler_params=_cparams(("arbitrary",)),
        name="ada",
    )(c_pad, w_ada, b_ada)


def _proj_kernel(x_ref, g_ref, sc_ref, sh_ref, w_ref, o_ref, h_scr):
    @pl.when(pl.program_id(1) == 0)
    def _():
        x = x_ref[...]
        ms = jnp.mean(x * x, axis=-1, keepdims=True)
        y = x * lax.rsqrt(ms + EPS) * g_ref[...]
        h_scr[...] = (y * (1.0 + sc_ref[0]) + sh_ref[0]).astype(BF16)

    o_ref[...] = jnp.dot(h_scr[...], w_ref[...], preferred_element_type=F32).astype(o_ref.dtype)


def _proj(x2, g, sc, sh, w_in_bf, seq):
    t, d = x2.shape
    n = w_in_bf.shape[1]
    tm = min(1024, seq)
    tn = 1024
    bpt = seq // tm
    return pl.pallas_call(
        _proj_kernel,
        grid=(t // tm, n // tn),
        in_specs=[pl.BlockSpec((tm, d), lambda i, j: (i, 0)),
                  pl.BlockSpec((1, d), lambda i, j: (0, 0)),
                  pl.BlockSpec((1, 1, d), lambda i, j: (i // bpt, 0, 0)),
                  pl.BlockSpec((1, 1, d), lambda i, j: (i // bpt, 0, 0)),
                  pl.BlockSpec((d, tn), lambda i, j: (0, j))],
        out_specs=pl.BlockSpec((tm, tn), lambda i, j: (i, j)),
        out_shape=jax.ShapeDtypeStruct((t, n), BF16),
        scratch_shapes=[pltpu.VMEM((tm, d), BF16)],
        compiler_params=_cparams(("arbitrary", "arbitrary")),
        name="proj",
    )(x2, g, sc, sh, w_in_bf)


def _moba_kernel(q_ref, k_ref, v_ref, o_ref, kaug_scr, kmean_scr, *, nb):
    blk = MOBA_BLOCK
    i = pl.program_id(2)
    seq = k_ref.shape[0]

    @pl.when(i == 0)
    def _():
        k = k_ref[...]
        kaug_scr[:, :LANES] = k
        row_blk = lax.broadcasted_iota(jnp.int32, (seq, LANES), 0) // blk
        lane = lax.broadcasted_iota(jnp.int32, (seq, LANES), 1)
        kaug_scr[:, LANES:] = jnp.where(lane == row_blk, 1.0, 0.0).astype(BF16)
        kmean_scr[...] = jnp.zeros_like(kmean_scr)
        km = jnp.mean(k.astype(F32).reshape(nb, blk, LANES), axis=1)
        kmean_scr[0:nb, :] = km

    q = q_ref[...]
    lane = lax.broadcasted_iota(jnp.int32, (blk, LANES), 1)
    gate = _dot_nt(q.astype(F32), kmean_scr[...], precision=HIGHEST)
    gate = jnp.where(lane < i, gate, -jnp.inf)
    sel = lane == i
    for _ in range(MOBA_TOPK):
        m = jnp.max(gate, axis=-1, keepdims=True)
        idx = jnp.min(jnp.where(gate == m, lane, LANES), axis=-1, keepdims=True)
        pick = lane == idx
        sel = sel | (pick & (lane < i))
        gate = jnp.where(pick, -jnp.inf, gate)
    bias = jnp.where(sel | (lane >= nb), 0.0, NEG_BIG).astype(BF16)
    q_aug = jnp.concatenate([q, bias], axis=-1)
    scale = A_HEAD_DIM ** -0.5

    def block_scores(j):
        kj = kaug_scr[pl.ds(pl.multiple_of(j * blk, blk), blk), :]
        return _dot_nt(q_aug, kj) * scale

    def online(carry, s, vj):
        m_prev, l_prev, acc = carry
        m_new = jnp.maximum(m_prev, jnp.max(s, axis=-1, keepdims=True))
        alpha = jnp.exp(m_prev - m_new)
        p = jnp.exp(s - m_new)
        l_new = alpha * l_prev + jnp.sum(p, axis=-1, keepdims=True)
        acc = alpha * acc + jnp.dot(p.astype(BF16), vj, preferred_element_type=F32)
        return m_new, l_new, acc

    def body(j, carry):
        vj = v_ref[pl.ds(pl.multiple_of(j * blk, blk), blk), :]
        return online(carry, block_scores(j), vj)

    init = (jnp.full((blk, 1), NEG_BIG, F32), jnp.zeros((blk, 1), F32), jnp.zeros((blk, LANES), F32))
    carry = lax.fori_loop(0, i, body, init)
    s = block_scores(i)
    r = lax.broadcasted_iota(jnp.int32, (blk, blk), 0)
    c = lax.broadcasted_iota(jnp.int32, (blk, blk), 1)
    s = jnp.where(c <= r, s, NEG_BIG)
    vi = v_ref[pl.ds(pl.multiple_of(i * blk, blk), blk), :]
    _, l_fin, acc = online(carry, s, vi)
    o_ref[...] = (acc / l_fin).astype(o_ref.dtype)


def _moba(proj, batch, seq):
    nb = seq // MOBA_BLOCK
    blk = MOBA_BLOCK
    t = batch * seq
    return pl.pallas_call(
        functools.partial(_moba_kernel, nb=nb),
        grid=(batch, A_HEADS, nb),
        in_specs=[pl.BlockSpec((blk, LANES), lambda b, h, i: (b * nb + i, CB_AQ + h)),
                  pl.BlockSpec((seq, LANES), lambda b, h, i: (b, CB_AK + h)),
                  pl.BlockSpec((seq, LANES), lambda b, h, i: (b, CB_AV + h))],
        out_specs=pl.BlockSpec((blk, LANES), lambda b, h, i: (b * nb + i, h)),
        out_shape=jax.ShapeDtypeStruct((t, A_WIDTH), BF16),
        scratch_shapes=[pltpu.VMEM((seq, 2 * LANES), BF16), pltpu.VMEM((LANES, LANES), F32)],
        compiler_params=_cparams(("arbitrary", "arbitrary", "arbitrary")),
        name="moba",
    )(proj, proj, proj)


def _hgrn_kernel(q_ref, f_ref, i_ref, og_ref, rl_ref, g_ref, o_ref, st_scr, b_scr, *, layer):
    ch, sub = HGRN_CHUNK, HGRN_SUB
    ns = ch // sub

    @pl.when(pl.program_id(2) == 0)
    def _():
        st_scr[...] = jnp.zeros_like(st_scr)

    rl = rl_ref[...]
    e = jnp.exp(rl - jnp.max(rl, axis=0, keepdims=True))
    lb = jnp.sum(e[: layer + 1], axis=0, keepdims=True) / jnp.sum(e, axis=0, keepdims=True)

    q = q_ref[...].astype(F32)
    x = f_ref[...].astype(F32)
    v_bf = i_ref[...]
    v = v_bf.astype(F32)
    f = lb + (1.0 - lb) * jax.nn.sigmoid(x)
    kin = (1.0 - lb) * jax.nn.sigmoid(-x)
    row = lax.broadcasted_iota(jnp.int32, (ch, LANES), 0)

    b = jnp.log(f)
    step = 1
    while step < ch:
        b = b + jnp.where(row >= step, pltpu.roll(b, step, 0), 0.0)
        step *= 2
    b_scr[...] = b
    r_end = b_scr[pl.ds(sub - 1, ns, stride=sub), :]
    r_sub = jnp.broadcast_to(r_end[:, None, :], (ns, sub, LANES)).reshape(ch, LANES)
    r_last = r_end[ns - 1:ns, :]

    kh = kin * jnp.exp(r_sub - b)
    sub_id = row // sub

    a_off = jnp.zeros((ch, ch), F32)
    for j in range(ns - 1):
        qj = q * jnp.exp(jnp.where(row >= sub * (j + 1), b - r_end[j:j + 1, :], -jnp.inf))
        kj = jnp.where(sub_id == j, kh, 0.0)
        a_off = a_off + _dot_nt(qj.astype(BF16), kj.astype(BF16))

    st = st_scr[...]
    o = jnp.dot(a_off.astype(BF16), v_bf, preferred_element_type=F32)
    o = o + _dot_nt((q * jnp.exp(b)).astype(BF16), st.astype(BF16))

    pos = row % sub
    o = o + jnp.sum(q * kin, axis=-1, keepdims=True) * v
    for delta in range(1, sub):
        kr = pltpu.roll(kin, delta, 0)
        br = pltpu.roll(b, delta, 0)
        vr = pltpu.roll(v, delta, 0)
        w = jnp.where(pos >= delta, q * kr * jnp.exp(b - br), 0.0)
        o = o + jnp.sum(w, axis=-1, keepdims=True) * vr

    ke = kh * jnp.exp(r_last - r_sub)
    st_scr[...] = jnp.exp(r_last) * st + _dot_tn(v_bf, ke.astype(BF16))

    y = o * lax.rsqrt(jnp.mean(o * o, axis=-1, keepdims=True) + EPS) * g_ref[...]
    og = og_ref[...].astype(F32)
    o_ref[...] = (y * (og * jax.nn.sigmoid(og))).astype(o_ref.dtype)


def _hgrn(proj, r_lower, r_norm_g, batch, seq, layer):
    ch = HGRN_CHUNK
    nc = seq // ch
    t = batch * seq
    nl = r_lower.shape[0]

    def col(cb):
        return pl.BlockSpec((ch, LANES), lambda b, h, c: (b * nc + c, cb + h))

    return pl.pallas_call(
        functools.partial(_hgrn_kernel, layer=layer),
        grid=(batch, R_HEADS, nc),
        in_specs=[col(CB_RQ), col(CB_RF), col(CB_RI), col(CB_ROG),
                  pl.BlockSpec((nl, LANES), lambda b, h, c: (0, h)),
                  pl.BlockSpec((1, LANES), lambda b, h, c: (0, 0))],
        out_specs=pl.BlockSpec((ch, LANES), lambda b, h, c: (b * nc + c, h)),
        out_shape=jax.ShapeDtypeStruct((t, R_WIDTH), BF16),
        scratch_shapes=[pltpu.VMEM((LANES, LANES), F32), pltpu.VMEM((ch, LANES), F32)],
        compiler_params=_cparams(("arbitrary", "arbitrary", "arbitrary")),
        name="hgrn",
    )(proj, proj, proj, proj, r_lower, r_norm_g)


def _merge_kernel(att_ref, or_ref, ga0_ref, ga1_ref, gr0_ref, gr1_ref, x_ref, gt_ref,
                  wua_ref, wur_ref, wo_ref, g2_ref, sc_ref, sh_ref, wr_ref, br_ref,
                  x1_ref, h2_ref, lg_ref):
    ya = jnp.dot(att_ref[...], wua_ref[...], preferred_element_type=F32)
    yr = jnp.dot(or_ref[...], wur_ref[...], preferred_element_type=F32)
    ga = jnp.concatenate([ga0_ref[...], ga1_ref[...]], axis=-1).astype(F32)
    gr = jnp.concatenate([gr0_ref[...], gr1_ref[...]], axis=-1).astype(F32)
    merged = jax.nn.sigmoid(ga) * ya + jax.nn.sigmoid(gr) * yr
    out = jnp.dot(merged.astype(BF16), wo_ref[...], preferred_element_type=F32)
    x1 = x_ref[...] + gt_ref[0] * out
    x1_ref[...] = x1
    ms = jnp.mean(x1 * x1, axis=-1, keepdims=True)
    h2 = (x1 * lax.rsqrt(ms + EPS) * g2_ref[...]) * (1.0 + sc_ref[0]) + sh_ref[0]
    h2_ref[...] = h2
    lg_ref[...] = jnp.dot(h2, wr_ref[...], preferred_element_type=F32, precision=HIGHEST) + br_ref[...]


def _merge(att, o_r, proj, x2, gt1, w_up_a, w_up_r, w_out, g2, sc2, sh2, w_router, b_router, seq):
    t, d = x2.shape
    tm = 256
    bpt = seq // tm
    half = d // 2
    cb = half // LANES

    def gspec(cb0, k):
        return pl.BlockSpec((tm, half), lambda i: (i, cb0 // cb + k))

    def const(shape):
        return pl.BlockSpec(shape, lambda i: tuple(0 for _ in shape), pipeline_mode=pl.Buffered(1))

    def per_batch():
        return pl.BlockSpec((1, 1, d), lambda i: (i // bpt, 0, 0))

    return pl.pallas_call(
        _merge_kernel,
        grid=(t // tm,),
        in_specs=[pl.BlockSpec((tm, A_WIDTH), lambda i: (i, 0)),
                  pl.BlockSpec((tm, R_WIDTH), lambda i: (i, 0)),
                  gspec(CB_GA, 0), gspec(CB_GA, 1), gspec(CB_GR, 0), gspec(CB_GR, 1),
                  pl.BlockSpec((tm, d), lambda i: (i, 0)),
                  per_batch(),
                  const((A_WIDTH, d)), const((R_WIDTH, d)), const((d, d)),
                  const((1, d)), per_batch(), per_batch(),
                  const((d, LANES)), const((1, LANES))],
        out_specs=[pl.BlockSpec((tm, d), lambda i: (i, 0)),
                   pl.BlockSpec((tm, d), lambda i: (i, 0)),
                   pl.BlockSpec((tm, LANES), lambda i: (i, 0))],
        out_shape=[jax.ShapeDtypeStruct((t, d), F32),
                   jax.ShapeDtypeStruct((t, d), F32),
                   jax.ShapeDtypeStruct((t, LANES), F32)],
        compiler_params=_cparams(("arbitrary",)),
        name="merge",
    )(att, o_r, proj, proj, proj, proj, x2, gt1, w_up_a, w_up_r, w_out, g2, sc2, sh2, w_router, b_router)


RT_E0, RT_E1, RT_W0, RT_W1, RT_R0, RT_R1 = 0, 1, 2, 3, 4, 5


def _route_kernel(lg_ref, rt_ref, cnt_ref, carry_scr):
    tr = lg_ref.shape[0]

    @pl.when(pl.program_id(0) == 0)
    def _():
        carry_scr[...] = jnp.zeros_like(carry_scr)

    x = lg_ref[...]
    lane = lax.broadcasted_iota(jnp.int32, (tr, LANES), 1)
    ninf = -jnp.inf

    def lane_max(val):
        return jnp.max(val, axis=-1, keepdims=True)

    def first_lane(mask):
        return jnp.min(jnp.where(mask, lane, LANES), axis=-1, keepdims=True)

    is_g = lane < N_GROUPS
    gmax = lane_max(jnp.where(is_g, x, ninf))
    grp = first_lane(is_g & (x == gmax))
    eg = jnp.where(is_g, jnp.exp(x - gmax), 0.0)
    pg_top = 1.0 / jnp.sum(eg, axis=-1, keepdims=True)

    lo = N_GROUPS + grp * EXPERTS_PER_GROUP
    is_e = (lane >= lo) & (lane < lo + EXPERTS_PER_GROUP)
    emax = lane_max(jnp.where(is_e, x, ninf))
    ee = jnp.where(is_e, jnp.exp(x - emax), 0.0)
    pe = ee / jnp.sum(ee, axis=-1, keepdims=True)
    pe = jnp.where(is_e, pe, ninf)
    p0 = lane_max(pe)
    l0 = first_lane(pe == p0)
    pe1 = jnp.where(lane == l0, ninf, pe)
    p1 = lane_max(pe1)
    l1 = first_lane(pe1 == p1)
    den = p0 + p1
    w0 = pg_top * p0 / den
    w1 = pg_top * p1 / den
    e0 = l0 - N_GROUPS
    e1 = l1 - N_GROUPS

    onehot = ((lane == e0) | (lane == e1)).astype(BF16)
    r = lax.broadcasted_iota(jnp.int32, (tr, tr), 0)
    c = lax.broadcasted_iota(jnp.int32, (tr, tr), 1)
    tri = jnp.where(c < r, 1.0, 0.0).astype(BF16)
    prefix = jnp.dot(tri, onehot, preferred_element_type=F32) + carry_scr[0:1, :]
    rank0 = jnp.sum(jnp.where(lane == e0, prefix, 0.0), axis=-1, keepdims=True)
    rank1 = jnp.sum(jnp.where(lane == e1, prefix, 0.0), axis=-1, keepdims=True)
    total = carry_scr[0:1, :] + jnp.sum(onehot.astype(F32), axis=0, keepdims=True)
    carry_scr[...] = jnp.broadcast_to(total, carry_scr.shape)
    cnt_ref[...] = jnp.broadcast_to(total, cnt_ref.shape)

    rec = jnp.zeros((tr, LANES), F32)
    for k, val in ((RT_E0, e0.astype(F32)), (RT_E1, e1.astype(F32)), (RT_W0, w0), (RT_W1, w1),
                   (RT_R0, rank0), (RT_R1, rank1)):
        rec = jnp.where(lane == k, val, rec)
    rt_ref[...] = rec


def _route(logits):
    t = logits.shape[0]
    tr = 512
    return pl.pallas_call(
        _route_kernel,
        grid=(t // tr,),
        in_specs=[pl.BlockSpec((tr, LANES), lambda i: (i, 0))],
        out_specs=[pl.BlockSpec((tr, LANES), lambda i: (i, 0)),
                   pl.BlockSpec((8, LANES), lambda i: (0, 0))],
        out_shape=[jax.ShapeDtypeStruct((t, LANES), F32),
                   jax.ShapeDtypeStruct((8, LANES), F32)],
        scratch_shapes=[pltpu.VMEM((8, LANES), F32)],
        compiler_params=_cparams(("arbitrary",)),
        name="route",
    )(logits)


def _row_copy(src_hbm, dst_buf, slot, src_row, dst_row, sem):
    return pltpu.make_async_copy(src_hbm.at[pl.ds(src_row, 1), :],
                                 dst_buf.at[slot, pl.ds(dst_row, 1), :],
                                 sem.at[slot])


def _moe_kernel(tok_ref, be_ref, nu_ref, h_hbm, w1_ref, w3_ref, w2_ref, y_ref, xbuf, sem):
    del be_ref
    rows = MOE_BLOCK
    i = pl.program_id(0)
    nused = nu_ref[0]

    def gather(block, slot, start):
        base = block * rows

        def body(r, _):
            cp = _row_copy(h_hbm, xbuf, slot, tok_ref[base + r], r, sem)
            if start:
                cp.start()
            else:
                cp.wait()
            return 0

        lax.fori_loop(0, rows, body, 0)

    @pl.when((i == 0) & (nused > 0))
    def _():
        gather(0, 0, True)

    @pl.when(i + 1 < nused)
    def _():
        gather(i + 1, (i + 1) % 2, True)

    @pl.when(i < nused)
    def _():
        slot = i % 2
        gather(i, slot, False)
        xb = xbuf[slot].astype(BF16)
        h1 = jnp.dot(xb, w1_ref[0], preferred_element_type=F32)
        h3 = jnp.dot(xb, w3_ref[0], preferred_element_type=F32)
        hid = (h1 * jax.nn.sigmoid(h1)) * h3
        y_ref[...] = jnp.dot(hid.astype(BF16), w2_ref[0], preferred_element_type=F32)

    @pl.when(i >= nused)
    def _():
        y_ref[...] = jnp.zeros_like(y_ref)


def _moe(row_tok, blk_e, nused, h2, w1_bf, w3_bf, w2_bf):
    t, d = h2.shape
    r = row_tok.shape[0]
    nblk = r // MOE_BLOCK
    grid_spec = pltpu.PrefetchScalarGridSpec(
        num_scalar_prefetch=3,
        grid=(nblk,),
        in_specs=[pl.BlockSpec(memory_space=pl.ANY),
                  pl.BlockSpec((1, d, D_EXPERT), lambda i, tok, be, nu: (be[i], 0, 0)),
                  pl.BlockSpec((1, d, D_EXPERT), lambda i, tok, be, nu: (be[i], 0, 0)),
                  pl.BlockSpec((1, D_EXPERT, d), lambda i, tok, be, nu: (be[i], 0, 0))],
        out_specs=pl.BlockSpec((MOE_BLOCK, d), lambda i, tok, be, nu: (i, 0)),
        scratch_shapes=[pltpu.VMEM((2, MOE_BLOCK, d), F32), pltpu.SemaphoreType.DMA((2,))],
    )
    return pl.pallas_call(
        _moe_kernel,
        grid_spec=grid_spec,
        out_shape=jax.ShapeDtypeStruct((r, d), F32),
        compiler_params=_cparams(("arbitrary",)),
        name="moe",
    )(row_tok, blk_e, nused, h2, w1_bf, w3_bf, w2_bf)


def _final_kernel(dest_ref, y_hbm, x1_ref, rt_ref, gt_ref, g_ref, o_ref, ybuf, sem, *, last_layer):
    tf = x1_ref.shape[0]
    i = pl.program_id(0)
    n = pl.num_programs(0)

    def gather(step, slot, start):
        base = step * tf * TOP_K_INNER

        def body(r, _):
            for k in range(TOP_K_INNER):
                cp = pltpu.make_async_copy(
                    y_hbm.at[pl.ds(dest_ref[base + r * TOP_K_INNER + k], 1), :],
                    ybuf.at[slot, k, pl.ds(r, 1), :],
                    sem.at[slot])
                if start:
                    cp.start()
                else:
                    cp.wait()
            return 0

        lax.fori_loop(0, tf, body, 0)

    @pl.when(i == 0)
    def _():
        gather(0, 0, True)

    @pl.when(i + 1 < n)
    def _():
        gather(i + 1, (i + 1) % 2, True)

    slot = i % 2
    gather(i, slot, False)
    rt = rt_ref[...]
    w0 = rt[:, RT_W0:RT_W0 + 1]
    w1 = rt[:, RT_W1:RT_W1 + 1]
    y = ybuf[slot, 0] * w0 + ybuf[slot, 1] * w1
    x2 = x1_ref[...] + gt_ref[0] * y
    if last_layer:
        ms = jnp.mean(x2 * x2, axis=-1, keepdims=True)
        x2 = x2 * lax.rsqrt(ms + EPS) * g_ref[...]
    o_ref[...] = x2


def _final(dest, yr, x1, rt, gt2, final_g, seq, last_layer):
    t, d = x1.shape
    tf = 256
    bpt = seq // tf
    grid_spec = pltpu.PrefetchScalarGridSpec(
        num_scalar_prefetch=1,
        grid=(t // tf,),
        in_specs=[pl.BlockSpec(memory_space=pl.ANY),
                  pl.BlockSpec((tf, d), lambda i, dst: (i, 0)),
                  pl.BlockSpec((tf, LANES), lambda i, dst: (i, 0)),
                  pl.BlockSpec((1, 1, d), lambda i, dst: (i // bpt, 0, 0)),
                  pl.BlockSpec((1, d), lambda i, dst: (0, 0))],
        out_specs=pl.BlockSpec((tf, d), lambda i, dst: (i, 0)),
        scratch_shapes=[pltpu.VMEM((2, TOP_K_INNER, tf, d), F32), pltpu.SemaphoreType.DMA((2,))],
    )
    return pl.pallas_call(
        functools.partial(_final_kernel, last_layer=last_layer),
        grid_spec=grid_spec,
        out_shape=jax.ShapeDtypeStruct((t, d), F32),
        compiler_params=_cparams(("arbitrary",)),
        name="final",
    )(dest, yr, x1, rt, gt2, final_g)


def _dispatch_tables(rt, counts_row):
    t = rt.shape[0]
    tk = t * TOP_K_INNER
    eid = rt[:, RT_E0:RT_E1 + 1].astype(jnp.int32)
    rank = rt[:, RT_R0:RT_R1 + 1].astype(jnp.int32)
    counts = counts_row[:N_EXPERTS].astype(jnp.int32)
    padded = (counts + MOE_BLOCK - 1) // MOE_BLOCK * MOE_BLOCK
    pad_end = jnp.cumsum(padded)
    pad_start = pad_end - padded
    dest = (pad_start[eid] + rank).reshape(tk)
    r = tk + N_EXPERTS * MOE_BLOCK
    tok = jnp.repeat(jnp.arange(t, dtype=jnp.int32), TOP_K_INNER)
    row_tok = jnp.zeros((r,), jnp.int32).at[dest].set(tok)
    nblk = r // MOE_BLOCK
    blk_e = jnp.minimum(jnp.searchsorted(pad_end, jnp.arange(nblk) * MOE_BLOCK, side='right'),
                        N_EXPERTS - 1).astype(jnp.int32)
    nused = (pad_end[-1:] // MOE_BLOCK).astype(jnp.int32)
    return dest.astype(jnp.int32), row_tok, blk_e, nused


def kernel(x, c, norm1_g, norm2_g, final_g, w_ada, b_ada, w_in, r_lower, r_norm_g,
           w_up_a, w_up_r, w_out, w_rg, b_rg, w_re, b_re, w1, w3, w2):
    batch, seq, d = x.shape
    t = batch * seq
    depth = w_in.shape[0]
    x2 = x.reshape(t, d)
    c_pad = jnp.zeros((8, d), F32).at[:batch].set(c)
    for l in range(depth):
        mod = _ada(c_pad, w_ada[l], b_ada[l][None, :])[:batch]
        sh1, sc1, gt1, sh2, sc2, gt2 = [m[:, None, :] for m in jnp.split(mod, 6, axis=-1)]
        proj = _proj(x2, norm1_g[l][None, :], sc1, sh1, w_in[l].astype(BF16), seq)
        att = _moba(proj, batch, seq)
        o_r = _hgrn(proj, r_lower, r_norm_g[l][None, :], batch, seq, l)
        w_router = jnp.zeros((d, LANES), F32).at[:, :N_GROUPS].set(w_rg[l]) \
            .at[:, N_GROUPS:N_GROUPS + N_EXPERTS].set(w_re[l])
        b_router = jnp.zeros((1, LANES), F32).at[0, :N_GROUPS].set(b_rg[l]) \
            .at[0, N_GROUPS:N_GROUPS + N_EXPERTS].set(b_re[l])
        x1, h2, logits = _merge(att, o_r, proj, x2, gt1, w_up_a[l].astype(BF16), w_up_r[l].astype(BF16),
                                w_out[l].astype(BF16), norm2_g[l][None, :], sc2, sh2, w_router, b_router, seq)
        rt, cnt = _route(logits)
        dest, row_tok, blk_e, nused = _dispatch_tables(rt, cnt[0])
        yr = _moe(row_tok, blk_e, nused, h2, w1[l].astype(BF16), w3[l].astype(BF16), w2[l].astype(BF16))
        x2 = _final(dest, yr, x1, rt, gt2, final_g[None, :], seq, l == depth - 1)
    return x2.reshape(batch, seq, d)
```

```python
import functools

import jax
import jax.numpy as jnp
from jax import lax
from jax.experimental import pallas as pl
from jax.experimental.pallas import tpu as pltpu

F32 = jnp.float32
BF16 = jnp.bfloat16
HIGHEST = lax.Precision.HIGHEST

D_MODEL = 2048
A_HEADS = 8
A_HEAD_DIM = 128
A_WIDTH = A_HEADS * A_HEAD_DIM
MOBA_BLOCK = 256
MOBA_TOPK = 3
R_HEADS = 8
R_KEY_DIM = 128
R_VAL_DIM = 128
R_WIDTH = R_HEADS * R_KEY_DIM
N_GROUPS = 4
EXPERTS_PER_GROUP = 8
N_EXPERTS = N_GROUPS * EXPERTS_PER_GROUP
TOP_K_INNER = 2
D_EXPERT = 512
MOE_BLOCK = 256
EPS = 1e-6
IN_COLS = 3 * A_WIDTH + 4 * R_WIDTH + 2 * D_MODEL

LANES = 128
CB_AQ = 0
CB_AK = CB_AQ + A_WIDTH // LANES
CB_AV = CB_AK + A_WIDTH // LANES
CB_RQ = CB_AV + A_WIDTH // LANES
CB_RF = CB_RQ + R_WIDTH // LANES
CB_RI = CB_RF + R_WIDTH // LANES
CB_ROG = CB_RI + R_WIDTH // LANES
CB_GA = CB_ROG + R_WIDTH // LANES
CB_GR = CB_GA + D_MODEL // LANES

MOBA_GROUP = 4
Q_PRESCALE = A_HEAD_DIM ** -0.5 * 1.4426950408889634
HGRN_CHUNK = 128
HGRN_SUB = 16
NEG_BIG = -1e30
VMEM_LIMIT = 56 * 1024 * 1024


def _cparams(sem):
    return pltpu.CompilerParams(dimension_semantics=sem, vmem_limit_bytes=VMEM_LIMIT)


def _dot_nt(a, b, **kw):
    return lax.dot_general(a, b, (((1,), (1,)), ((), ())), preferred_element_type=F32, **kw)


def _dot_tn(a, b):
    return lax.dot_general(a, b, (((0,), (0,)), ((), ())), preferred_element_type=F32)


def _ada_kernel(c_ref, w_ref, b_ref, o_ref):
    c = c_ref[...]
    ca = c * jax.nn.sigmoid(c)
    o_ref[...] = jnp.dot(ca, w_ref[...], preferred_element_type=F32, precision=HIGHEST) + b_ref[...]


def _ada(c_pad, w_ada, b_ada):
    rows, d = c_pad.shape
    n = w_ada.shape[1]
    tn = 1024
    return pl.pallas_call(
        _ada_kernel,
        grid=(n // tn,),
        in_specs=[pl.BlockSpec((rows, d), lambda j: (0, 0)),
                  pl.BlockSpec((d, tn), lambda j: (0, j)),
                  pl.BlockSpec((1, tn), lambda j: (0, j))],
        out_specs=pl.BlockSpec((rows, tn), lambda j: (0, j)),
        out_shape=jax.ShapeDtypeStruct((rows, n), F32),
        compiler_params=_cparams(("arbitrary",)),
        name="ada",
    )(c_pad, w_ada, b_ada)


def _proj_kernel(x_ref, g_ref, sc_ref, sh_ref, w_ref, cs_ref, o_ref, h_scr):
    @pl.when(pl.program_id(1) == 0)
    def _():
        x = x_ref[...]
        ms = jnp.mean(x * x, axis=-1, keepdims=True)
        y = x * lax.rsqrt(ms + EPS) * g_ref[...]
        h_scr[...] = (y * (1.0 + sc_ref[0]) + sh_ref[0]).astype(BF16)

    acc = jnp.dot(h_scr[...], w_ref[...], preferred_element_type=F32)
    o_ref[...] = (acc * cs_ref[...]).astype(o_ref.dtype)


def _proj(x2, g, sc, sh, w_in_bf, seq):
    t, d = x2.shape
    n = w_in_bf.shape[1]
    tm = min(1024, seq)
    tn = 1024
    bpt = seq // tm
    col_scale = jnp.ones((1, n), F32).at[:, CB_AQ * LANES:CB_AK * LANES].set(Q_PRESCALE)
    return pl.pallas_call(
        _proj_kernel,
        grid=(t // tm, n // tn),
        in_specs=[pl.BlockSpec((tm, d), lambda i, j: (i, 0)),
                  pl.BlockSpec((1, d), lambda i, j: (0, 0)),
                  pl.BlockSpec((1, 1, d), lambda i, j: (i // bpt, 0, 0)),
                  pl.BlockSpec((1, 1, d), lambda i, j: (i // bpt, 0, 0)),
                  pl.BlockSpec((d, tn), lambda i, j: (0, j)),
                  pl.BlockSpec((1, tn), lambda i, j: (0, j))],
        out_specs=pl.BlockSpec((tm, tn), lambda i, j: (i, j)),
        out_shape=jax.ShapeDtypeStruct((t, n), BF16),
        scratch_shapes=[pltpu.VMEM((tm, d), BF16)],
        compiler_params=_cparams(("arbitrary", "arbitrary")),
        name="proj",
    )(x2, g, sc, sh, w_in_bf, col_scale)


def _moba_kernel(q_ref, k_ref, v_ref, o_ref, kaug_scr, vt_scr, kmean_scr, s_scr, *, nb, grp, gr):
    blk = MOBA_BLOCK
    gk = grp * blk
    i = pl.program_id(2)
    seq = k_ref.shape[0]

    @pl.when(i == 0)
    def _():
        k = k_ref[...]
        kaug_scr[:, :LANES] = k
        row_blk = lax.broadcasted_iota(jnp.int32, (seq, LANES), 0) // blk
        lane = lax.broadcasted_iota(jnp.int32, (seq, LANES), 1)
        kaug_scr[:, LANES:] = jnp.where(lane == row_blk, 1.0, 0.0).astype(BF16)
        kmean_scr[...] = jnp.zeros_like(kmean_scr)
        kmean_scr[0:nb, :] = jnp.mean(k.astype(F32).reshape(nb, blk, LANES), axis=1)
        ones_row = jnp.where(lax.broadcasted_iota(jnp.int32, (16, gk), 0) == 0, 1.0, 0.0).astype(BF16)
        for j in range(nb):
            vj = v_ref[j * blk:(j + 1) * blk, :].astype(F32)
            vt_scr[j // grp, 0:LANES, (j % grp) * blk:(j % grp + 1) * blk] = vj.T.astype(BF16)
        for g in range(nb // grp):
            vt_scr[g, LANES:LANES + 16, :] = ones_row

    q = q_ref[...]
    gate = _dot_nt(kmean_scr[...], q.astype(F32), precision=HIGHEST)
    rid = lax.broadcasted_iota(jnp.int32, (gr, blk), 0)
    gate = jnp.where(rid < i, gate, -jnp.inf)
    sel = rid == i
    for _ in range(MOBA_TOPK):
        m = jnp.max(gate, axis=0, keepdims=True)
        idx = jnp.min(jnp.where(gate == m, rid, gr), axis=0, keepdims=True)
        pick = rid == idx
        sel = sel | (pick & (rid < i))
        gate = jnp.where(pick, -jnp.inf, gate)
    bias_t = jnp.where(sel, 0.0, NEG_BIG)
    bias_t = jnp.concatenate([bias_t, jnp.zeros((LANES - gr, blk), F32)], axis=0)
    q_aug = jnp.concatenate([q, bias_t.T.astype(BF16)], axis=-1)

    qpos = i * blk + lax.broadcasted_iota(jnp.int32, (gk, blk), 1)

    def score_group(g, causal):
        kg = kaug_scr[pl.ds(pl.multiple_of(g * gk, gk), gk), :]
        s = _dot_nt(kg, q_aug)
        if causal:
            kpos = g * gk + lax.broadcasted_iota(jnp.int32, (gk, blk), 0)
            s = jnp.where(kpos <= qpos, s, NEG_BIG)
        s_scr[g] = s
        return jnp.max(s.reshape(gk // 8, 8, blk), axis=0)

    last = i // grp
    mx = lax.fori_loop(0, last, lambda g, mx: jnp.maximum(mx, score_group(g, False)),
                       jnp.full((8, blk), NEG_BIG, F32))
    mx = jnp.maximum(mx, score_group(last, True))
    m = jnp.max(mx, axis=0, keepdims=True)

    def pv_group(g, acc):
        p = jnp.exp2(s_scr[g] - m).astype(BF16)
        return acc + jnp.dot(vt_scr[g], p, preferred_element_type=F32)

    acc = lax.fori_loop(0, last + 1, pv_group, jnp.zeros((LANES + 16, blk), F32))
    out_t = acc[0:LANES] / acc[LANES:LANES + 1]
    o_ref[...] = out_t.T.astype(o_ref.dtype)


def _moba(proj, batch, seq):
    nb = seq // MOBA_BLOCK
    blk = MOBA_BLOCK
    grp = min(MOBA_GROUP, nb)
    gr = -(-nb // 8) * 8
    t = batch * seq
    assert nb % grp == 0 and gr <= LANES
    return pl.pallas_call(
        functools.partial(_moba_kernel, nb=nb, grp=grp, gr=gr),
        grid=(batch, A_HEADS, nb),
        in_specs=[pl.BlockSpec((blk, LANES), lambda b, h, i: (b * nb + i, CB_AQ + h)),
                  pl.BlockSpec((seq, LANES), lambda b, h, i: (b, CB_AK + h)),
                  pl.BlockSpec((seq, LANES), lambda b, h, i: (b, CB_AV + h))],
        out_specs=pl.BlockSpec((blk, LANES), lambda b, h, i: (b * nb + i, h)),
        out_shape=jax.ShapeDtypeStruct((t, A_WIDTH), BF16),
        scratch_shapes=[pltpu.VMEM((seq, 2 * LANES), BF16),
                        pltpu.VMEM((nb // grp, LANES + 16, grp * blk), BF16),
                        pltpu.VMEM((gr, LANES), F32),
                        pltpu.VMEM((nb // grp, grp * blk, blk), F32)],
        compiler_params=_cparams(("arbitrary", "arbitrary", "arbitrary")),
        name="moba",
    )(proj, proj, proj)


def _hgrn_kernel(q_ref, f_ref, i_ref, og_ref, rl_ref, g_ref, o_ref, st_scr, b_scr, *, layer):
    ch, sub = HGRN_CHUNK, HGRN_SUB
    ns = ch // sub

    @pl.when(pl.program_id(2) == 0)
    def _():
        st_scr[...] = jnp.zeros_like(st_scr)

    rl = rl_ref[...]
    e = jnp.exp(rl - jnp.max(rl, axis=0, keepdims=True))
    lb = jnp.sum(e[: layer + 1], axis=0, keepdims=True) / jnp.sum(e, axis=0, keepdims=True)

    q = q_ref[...].astype(F32)
    x = f_ref[...].astype(F32)
    v_bf = i_ref[...]
    v = v_bf.astype(F32)
    f = lb + (1.0 - lb) * jax.nn.sigmoid(x)
    kin = (1.0 - lb) * jax.nn.sigmoid(-x)
    row = lax.broadcasted_iota(jnp.int32, (ch, LANES), 0)

    b = jnp.log(f)
    step = 1
    while step < ch:
        b = b + jnp.where(row >= step, pltpu.roll(b, step, 0), 0.0)
        step *= 2
    b_scr[...] = b
    r_end = b_scr[pl.ds(sub - 1, ns, stride=sub), :]
    r_sub = jnp.broadcast_to(r_end[:, None, :], (ns, sub, LANES)).reshape(ch, LANES)
    r_last = r_end[ns - 1:ns, :]

    kh = kin * jnp.exp(r_sub - b)
    sub_id = row // sub

    a_off = jnp.zeros((ch, ch), F32)
    for j in range(ns - 1):
        qj = q * jnp.exp(jnp.where(row >= sub * (j + 1), b - r_end[j:j + 1, :], -jnp.inf))
        kj = jnp.where(sub_id == j, kh, 0.0)
        a_off = a_off + _dot_nt(qj.astype(BF16), kj.astype(BF16))

    st = st_scr[...]
    o = jnp.dot(a_off.astype(BF16), v_bf, preferred_element_type=F32)
    o = o + _dot_nt((q * jnp.exp(b)).astype(BF16), st.astype(BF16))

    pos = row % sub
    o = o + jnp.sum(q * kin, axis=-1, keepdims=True) * v
    for delta in range(1, sub):
        kr = pltpu.roll(kin, delta, 0)
        br = pltpu.roll(b, delta, 0)
        vr = pltpu.roll(v, delta, 0)
        w = jnp.where(pos >= delta, q * kr * jnp.exp(b - br), 0.0)
        o = o + jnp.sum(w, axis=-1, keepdims=True) * vr

    ke = kh * jnp.exp(r_last - r_sub)
    st_scr[...] = jnp.exp(r_last) * st + _dot_tn(v_bf, ke.astype(BF16))

    y = o * lax.rsqrt(jnp.mean(o * o, axis=-1, keepdims=True) + EPS) * g_ref[...]
    og = og_ref[...].astype(F32)
    o_ref[...] = (y * (og * jax.nn.sigmoid(og))).astype(o_ref.dtype)


def _hgrn(proj, r_lower, r_norm_g, batch, seq, layer):
    ch = HGRN_CHUNK
    nc = seq // ch
    t = batch * seq
    nl = r_lower.shape[0]

    def col(cb):
        return pl.BlockSpec((ch, LANES), lambda b, h, c: (b * nc + c, cb + h))

    return pl.pallas_call(
        functools.partial(_hgrn_kernel, layer=layer),
        grid=(batch, R_HEADS, nc),
        in_specs=[col(CB_RQ), col(CB_RF), col(CB_RI), col(CB_ROG),
                  pl.BlockSpec((nl, LANES), lambda b, h, c: (0, h)),
                  pl.BlockSpec((1, LANES), lambda b, h, c: (0, 0))],
        out_specs=pl.BlockSpec((ch, LANES), lambda b, h, c: (b * nc + c, h)),
        out_shape=jax.ShapeDtypeStruct((t, R_WIDTH), BF16),
        scratch_shapes=[pltpu.VMEM((LANES, LANES), F32), pltpu.VMEM((ch, LANES), F32)],
        compiler_params=_cparams(("arbitrary", "arbitrary", "arbitrary")),
        name="hgrn",
    )(proj, proj, proj, proj, r_lower, r_norm_g)


def _merge_kernel(att_ref, or_ref, ga0_ref, ga1_ref, gr0_ref, gr1_ref, x_ref, gt_ref,
                  wua_ref, wur_ref, wo_ref, g2_ref, sc_ref, sh_ref, wr_ref, br_ref,
                  x1_ref, h2_ref, lg_ref):
    ya = jnp.dot(att_ref[...], wua_ref[...], preferred_element_type=F32)
    yr = jnp.dot(or_ref[...], wur_ref[...], preferred_element_type=F32)
    ga = jnp.concatenate([ga0_ref[...], ga1_ref[...]], axis=-1).astype(F32)
    gr = jnp.concatenate([gr0_ref[...], gr1_ref[...]], axis=-1).astype(F32)
    merged = jax.nn.sigmoid(ga) * ya + jax.nn.sigmoid(gr) * yr
    out = jnp.dot(merged.astype(BF16), wo_ref[...], preferred_element_type=F32)
    x1 = x_ref[...] + gt_ref[0] * out
    x1_ref[...] = x1
    ms = jnp.mean(x1 * x1, axis=-1, keepdims=True)
    h2 = (x1 * lax.rsqrt(ms + EPS) * g2_ref[...]) * (1.0 + sc_ref[0]) + sh_ref[0]
    h2_ref[...] = h2
    lg_ref[...] = jnp.dot(h2, wr_ref[...], preferred_element_type=F32, precision=HIGHEST) + br_ref[...]


def _merge(att, o_r, proj, x2, gt1, w_up_a, w_up_r, w_out, g2, sc2, sh2, w_router, b_router, seq):
    t, d = x2.shape
    tm = 256
    bpt = seq // tm
    half = d // 2
    cb = half // LANES

    def gspec(cb0, k):
        return pl.BlockSpec((tm, half), lambda i: (i, cb0 // cb + k))

    def const(shape):
        return pl.BlockSpec(shape, lambda i: tuple(0 for _ in shape), pipeline_mode=pl.Buffered(1))

    def per_batch():
        return pl.BlockSpec((1, 1, d), lambda i: (i // bpt, 0, 0))

    return pl.pallas_call(
        _merge_kernel,
        grid=(t // tm,),
        in_specs=[pl.BlockSpec((tm, A_WIDTH), lambda i: (i, 0)),
                  pl.BlockSpec((tm, R_WIDTH), lambda i: (i, 0)),
                  gspec(CB_GA, 0), gspec(CB_GA, 1), gspec(CB_GR, 0), gspec(CB_GR, 1),
                  pl.BlockSpec((tm, d), lambda i: (i, 0)),
                  per_batch(),
                  const((A_WIDTH, d)), const((R_WIDTH, d)), const((d, d)),
                  const((1, d)), per_batch(), per_batch(),
                  const((d, LANES)), const((1, LANES))],
        out_specs=[pl.BlockSpec((tm, d), lambda i: (i, 0)),
                   pl.BlockSpec((tm, d), lambda i: (i, 0)),
                   pl.BlockSpec((tm, LANES), lambda i: (i, 0))],
        out_shape=[jax.ShapeDtypeStruct((t, d), F32),
                   jax.ShapeDtypeStruct((t, d), F32),
                   jax.ShapeDtypeStruct((t, LANES), F32)],
        compiler_params=_cparams(("arbitrary",)),
        name="merge",
    )(att, o_r, proj, proj, proj, proj, x2, gt1, w_up_a, w_up_r, w_out, g2, sc2, sh2, w_router, b_router)


RT_E0, RT_E1, RT_W0, RT_W1, RT_R0, RT_R1 = 0, 1, 2, 3, 4, 5


def _route_kernel(lg_ref, rt_ref, cnt_ref, carry_scr):
    tr = lg_ref.shape[0]

    @pl.when(pl.program_id(0) == 0)
    def _():
        carry_scr[...] = jnp.zeros_like(carry_scr)

    x = lg_ref[...]
    lane = lax.broadcasted_iota(jnp.int32, (tr, LANES), 1)
    ninf = -jnp.inf

    def lane_max(val):
        return jnp.max(val, axis=-1, keepdims=True)

    def first_lane(mask):
        return jnp.min(jnp.where(mask, lane, LANES), axis=-1, keepdims=True)

    is_g = lane < N_GROUPS
    gmax = lane_max(jnp.where(is_g, x, ninf))
    grp = first_lane(is_g & (x == gmax))
    eg = jnp.where(is_g, jnp.exp(x - gmax), 0.0)
    pg_top = 1.0 / jnp.sum(eg, axis=-1, keepdims=True)

    lo = N_GROUPS + grp * EXPERTS_PER_GROUP
    is_e = (lane >= lo) & (lane < lo + EXPERTS_PER_GROUP)
    emax = lane_max(jnp.where(is_e, x, ninf))
    ee = jnp.where(is_e, jnp.exp(x - emax), 0.0)
    pe = ee / jnp.sum(ee, axis=-1, keepdims=True)
    pe = jnp.where(is_e, pe, ninf)
    p0 = lane_max(pe)
    l0 = first_lane(pe == p0)
    pe1 = jnp.where(lane == l0, ninf, pe)
    p1 = lane_max(pe1)
    l1 = first_lane(pe1 == p1)
    den = p0 + p1
    w0 = pg_top * p0 / den
    w1 = pg_top * p1 / den
    e0 = l0 - N_GROUPS
    e1 = l1 - N_GROUPS

    onehot = ((lane == e0) | (lane == e1)).astype(BF16)
    r = lax.broadcasted_iota(jnp.int32, (tr, tr), 0)
    c = lax.broadcasted_iota(jnp.int32, (tr, tr), 1)
    tri = jnp.where(c < r, 1.0, 0.0).astype(BF16)
    prefix = jnp.dot(tri, onehot, preferred_element_type=F32) + carry_scr[0:1, :]
    rank0 = jnp.sum(jnp.where(lane == e0, prefix, 0.0), axis=-1, keepdims=True)
    rank1 = jnp.sum(jnp.where(lane == e1, prefix, 0.0), axis=-1, keepdims=True)
    total = carry_scr[0:1, :] + jnp.sum(onehot.astype(F32), axis=0, keepdims=True)
    carry_scr[...] = jnp.broadcast_to(total, carry_scr.shape)
    cnt_ref[...] = jnp.broadcast_to(total, cnt_ref.shape)

    rec = jnp.zeros((tr, LANES), F32)
    for k, val in ((RT_E0, e0.astype(F32)), (RT_E1, e1.astype(F32)), (RT_W0, w0), (RT_W1, w1),
                   (RT_R0, rank0), (RT_R1, rank1)):
        rec = jnp.where(lane == k, val, rec)
    rt_ref[...] = rec


def _route(logits):
    t = logits.shape[0]
    tr = 512
    return pl.pallas_call(
        _route_kernel,
        grid=(t // tr,),
        in_specs=[pl.BlockSpec((tr, LANES), lambda i: (i, 0))],
        out_specs=[pl.BlockSpec((tr, LANES), lambda i: (i, 0)),
                   pl.BlockSpec((8, LANES), lambda i: (0, 0))],
        out_shape=[jax.ShapeDtypeStruct((t, LANES), F32),
                   jax.ShapeDtypeStruct((8, LANES), F32)],
        scratch_shapes=[pltpu.VMEM((8, LANES), F32)],
        compiler_params=_cparams(("arbitrary",)),
        name="route",
    )(logits)


def _row_copy(src_hbm, dst_buf, slot, src_row, dst_row, sem):
    return pltpu.make_async_copy(src_hbm.at[pl.ds(src_row, 1), :],
                                 dst_buf.at[slot, pl.ds(dst_row, 1), :],
                                 sem.at[slot])


def _moe_kernel(tok_ref, be_ref, nu_ref, h_hbm, w1_ref, w3_ref, w2_ref, y_ref, xbuf, sem):
    del be_ref
    rows = MOE_BLOCK
    i = pl.program_id(0)
    nused = nu_ref[0]

    def gather(block, slot, start):
        base = block * rows

        def body(r, _):
            cp = _row_copy(h_hbm, xbuf, slot, tok_ref[base + r], r, sem)
            if start:
                cp.start()
            else:
                cp.wait()
            return 0

        lax.fori_loop(0, rows, body, 0, unroll=8)

    @pl.when((i == 0) & (nused > 0))
    def _():
        gather(0, 0, True)

    @pl.when(i + 1 < nused)
    def _():
        gather(i + 1, (i + 1) % 2, True)

    @pl.when(i < nused)
    def _():
        slot = i % 2
        gather(i, slot, False)
        xb = xbuf[slot].astype(BF16)
        h1 = jnp.dot(xb, w1_ref[0], preferred_element_type=F32)
        h3 = jnp.dot(xb, w3_ref[0], preferred_element_type=F32)
        hid = (h1 * jax.nn.sigmoid(h1)) * h3
        y_ref[...] = jnp.dot(hid.astype(BF16), w2_ref[0], preferred_element_type=F32)

    @pl.when(i >= nused)
    def _():
        y_ref[...] = jnp.zeros_like(y_ref)


def _moe(row_tok, blk_e, nused, h2, w1_bf, w3_bf, w2_bf):
    t, d = h2.shape
    r = row_tok.shape[0]
    nblk = r // MOE_BLOCK
    grid_spec = pltpu.PrefetchScalarGridSpec(
        num_scalar_prefetch=3,
        grid=(nblk,),
        in_specs=[pl.BlockSpec(memory_space=pl.ANY),
                  pl.BlockSpec((1, d, D_EXPERT), lambda i, tok, be, nu: (be[i], 0, 0)),
                  pl.BlockSpec((1, d, D_EXPERT), lambda i, tok, be, nu: (be[i], 0, 0)),
                  pl.BlockSpec((1, D_EXPERT, d), lambda i, tok, be, nu: (be[i], 0, 0))],
        out_specs=pl.BlockSpec((MOE_BLOCK, d), lambda i, tok, be, nu: (i, 0)),
        scratch_shapes=[pltpu.VMEM((2, MOE_BLOCK, d), F32), pltpu.SemaphoreType.DMA((2,))],
    )
    return pl.pallas_call(
        _moe_kernel,
        grid_spec=grid_spec,
        out_shape=jax.ShapeDtypeStruct((r, d), F32),
        compiler_params=_cparams(("arbitrary",)),
        name="moe",
    )(row_tok, blk_e, nused, h2, w1_bf, w3_bf, w2_bf)


def _final_kernel(dest_ref, y_hbm, x1_ref, rt_ref, gt_ref, g_ref, o_ref, ybuf, sem, *, last_layer):
    tf = x1_ref.shape[0]
    i = pl.program_id(0)
    n = pl.num_programs(0)

    def gather(step, slot, start):
        base = step * tf * TOP_K_INNER

        def body(r, _):
            for k in range(TOP_K_INNER):
                cp = pltpu.make_async_copy(
                    y_hbm.at[pl.ds(dest_ref[base + r * TOP_K_INNER + k], 1), :],
                    ybuf.at[slot, k, pl.ds(r, 1), :],
                    sem.at[slot])
                if start:
                    cp.start()
                else:
                    cp.wait()
            return 0

        lax.fori_loop(0, tf, body, 0, unroll=8)

    @pl.when(i == 0)
    def _():
        gather(0, 0, True)

    @pl.when(i + 1 < n)
    def _():
        gather(i + 1, (i + 1) % 2, True)

    slot = i % 2
    gather(i, slot, False)
    rt = rt_ref[...]
    w0 = rt[:, RT_W0:RT_W0 + 1]
    w1 = rt[:, RT_W1:RT_W1 + 1]
    y = ybuf[slot, 0] * w0 + ybuf[slot, 1] * w1
    x2 = x1_ref[...] + gt_ref[0] * y
    if last_layer:
        ms = jnp.mean(x2 * x2, axis=-1, keepdims=True)
        x2 = x2 * lax.rsqrt(ms + EPS) * g_ref[...]
    o_ref[...] = x2


def _final(dest, yr, x1, rt, gt2, final_g, seq, last_layer):
    t, d = x1.shape
    tf = 256
    bpt = seq // tf
    grid_spec = pltpu.PrefetchScalarGridSpec(
        num_scalar_prefetch=1,
        grid=(t // tf,),
        in_specs=[pl.BlockSpec(memory_space=pl.ANY),
                  pl.BlockSpec((tf, d), lambda i, dst: (i, 0)),
                  pl.BlockSpec((tf, LANES), lambda i, dst: (i, 0)),
                  pl.BlockSpec((1, 1, d), lambda i, dst: (i // bpt, 0, 0)),
                  pl.BlockSpec((1, d), lambda i, dst: (0, 0))],
        out_specs=pl.BlockSpec((tf, d), lambda i, dst: (i, 0)),
        scratch_shapes=[pltpu.VMEM((2, TOP_K_INNER, tf, d), F32), pltpu.SemaphoreType.DMA((2,))],
    )
    return pl.pallas_call(
        functools.partial(_final_kernel, last_layer=last_layer),
        grid_spec=grid_spec,
        out_shape=jax.ShapeDtypeStruct((t, d), F32),
        compiler_params=_cparams(("arbitrary",)),
        name="final",
    )(dest, yr, x1, rt, gt2, final_g)


def _dispatch_tables(rt, counts_row):
    t = rt.shape[0]
    tk = t * TOP_K_INNER
    eid = rt[:, RT_E0:RT_E1 + 1].astype(jnp.int32)
    rank = rt[:, RT_R0:RT_R1 + 1].astype(jnp.int32)
    counts = counts_row[:N_EXPERTS].astype(jnp.int32)
    padded = (counts + MOE_BLOCK - 1) // MOE_BLOCK * MOE_BLOCK
    pad_end = jnp.cumsum(padded)
    pad_start = pad_end - padded
    dest = (pad_start[eid] + rank).reshape(tk)
    r = tk + N_EXPERTS * MOE_BLOCK
    tok = jnp.repeat(jnp.arange(t, dtype=jnp.int32), TOP_K_INNER)
    row_tok = jnp.zeros((r,), jnp.int32).at[dest].set(tok)
    nblk = r // MOE_BLOCK
    blk_start = jnp.arange(nblk, dtype=jnp.int32) * MOE_BLOCK
    blk_e = jnp.minimum(jnp.sum(pad_end[None, :] <= blk_start[:, None], axis=1), N_EXPERTS - 1).astype(jnp.int32)
    nused = (pad_end[-1:] // MOE_BLOCK).astype(jnp.int32)
    return dest.astype(jnp.int32), row_tok, blk_e, nused


def kernel(x, c, norm1_g, norm2_g, final_g, w_ada, b_ada, w_in, r_lower, r_norm_g,
           w_up_a, w_up_r, w_out, w_rg, b_rg, w_re, b_re, w1, w3, w2):
    batch, seq, d = x.shape
    t = batch * seq
    depth = w_in.shape[0]
    x2 = x.reshape(t, d)
    c_pad = jnp.zeros((8, d), F32).at[:batch].set(c)
    for l in range(depth):
        mod = _ada(c_pad, w_ada[l], b_ada[l][None, :])[:batch]
        sh1, sc1, gt1, sh2, sc2, gt2 = [m[:, None, :] for m in jnp.split(mod, 6, axis=-1)]
        proj = _proj(x2, norm1_g[l][None, :], sc1, sh1, w_in[l].astype(BF16), seq)
        att = _moba(proj, batch, seq)
        o_r = _hgrn(proj, r_lower, r_norm_g[l][None, :], batch, seq, l)
        w_router = jnp.zeros((d, LANES), F32).at[:, :N_GROUPS].set(w_rg[l]) \
            .at[:, N_GROUPS:N_GROUPS + N_EXPERTS].set(w_re[l])
        b_router = jnp.zeros((1, LANES), F32).at[0, :N_GROUPS].set(b_rg[l]) \
            .at[0, N_GROUPS:N_GROUPS + N_EXPERTS].set(b_re[l])
        x1, h2, logits = _merge(att, o_r, proj, x2, gt1, w_up_a[l].astype(BF16), w_up_r[l].astype(BF16),
                                w_out[l].astype(BF16), norm2_g[l][None, :], sc2, sh2, w_router, b_router, seq)
        rt, cnt = _route(logits)
        dest, row_tok, blk_e, nused = _dispatch_tables(rt, cnt[0])
        yr = _moe(row_tok, blk_e, nused, h2, w1[l].astype(BF16), w3[l].astype(BF16), w2[l].astype(BF16))
        x2 = _final(dest, yr, x1, rt, gt2, final_g[None, :], seq, l == depth - 1)
    return x2.reshape(batch, seq, d)
```

```python
import functools

import jax
import jax.numpy as jnp
from jax import lax
from jax.experimental import pallas as pl
from jax.experimental.pallas import tpu as pltpu

F32 = jnp.float32
BF16 = jnp.bfloat16
HIGHEST = lax.Precision.HIGHEST

D_MODEL = 2048
A_HEADS = 8
A_HEAD_DIM = 128
A_WIDTH = A_HEADS * A_HEAD_DIM
MOBA_BLOCK = 256
MOBA_TOPK = 3
R_HEADS = 8
R_KEY_DIM = 128
R_VAL_DIM = 128
R_WIDTH = R_HEADS * R_KEY_DIM
N_GROUPS = 4
EXPERTS_PER_GROUP = 8
N_EXPERTS = N_GROUPS * EXPERTS_PER_GROUP
TOP_K_INNER = 2
D_EXPERT = 512
MOE_BLOCK = 256
EPS = 1e-6
IN_COLS = 3 * A_WIDTH + 4 * R_WIDTH + 2 * D_MODEL

LANES = 128
SUBLANES = 8
CB_AQ = 0
CB_AK = CB_AQ + A_WIDTH // LANES
CB_AV = CB_AK + A_WIDTH // LANES
CB_RQ = CB_AV + A_WIDTH // LANES
CB_RF = CB_RQ + R_WIDTH // LANES
CB_RI = CB_RF + R_WIDTH // LANES
CB_ROG = CB_RI + R_WIDTH // LANES
CB_GA = CB_ROG + R_WIDTH // LANES
CB_GR = CB_GA + D_MODEL // LANES

MOBA_GROUP = 4
Q_PRESCALE = A_HEAD_DIM ** -0.5 * 1.4426950408889634
HGRN_TILE = 512
HGRN_CHUNK = 32
HGRN_SUB = 4
NEG_BIG = -1e30
VMEM_LIMIT = 56 * 1024 * 1024


def _cparams(sem):
    return pltpu.CompilerParams(dimension_semantics=sem, vmem_limit_bytes=VMEM_LIMIT)


def _dot_nt(a, b, **kw):
    return lax.dot_general(a, b, (((1,), (1,)), ((), ())), preferred_element_type=F32, **kw)


def _dot_tn(a, b):
    return lax.dot_general(a, b, (((0,), (0,)), ((), ())), preferred_element_type=F32)


def _ada_kernel(c_ref, w_ref, b_ref, o_ref):
    c = c_ref[...]
    ca = c * jax.nn.sigmoid(c)
    o_ref[...] = jnp.dot(ca, w_ref[...], preferred_element_type=F32, precision=HIGHEST) + b_ref[...]


def _ada(c_pad, w_ada, b_ada):
    rows, d = c_pad.shape
    n = w_ada.shape[1]
    tn = 1024
    return pl.pallas_call(
        _ada_kernel,
        grid=(n // tn,),
        in_specs=[pl.BlockSpec((rows, d), lambda j: (0, 0)),
                  pl.BlockSpec((d, tn), lambda j: (0, j)),
                  pl.BlockSpec((1, tn), lambda j: (0, j))],
        out_specs=pl.BlockSpec((rows, tn), lambda j: (0, j)),
        out_shape=jax.ShapeDtypeStruct((rows, n), F32),
        compiler_params=_cparams(("arbitrary",)),
        name="ada",
    )(c_pad, w_ada, b_ada)


def _proj_kernel(x_ref, g_ref, sc_ref, sh_ref, w_ref, cs_ref, o_ref, h_scr):
    @pl.when(pl.program_id(1) == 0)
    def _():
        x = x_ref[...]
        ms = jnp.mean(x * x, axis=-1, keepdims=True)
        y = x * lax.rsqrt(ms + EPS) * g_ref[...]
        h_scr[...] = (y * (1.0 + sc_ref[0]) + sh_ref[0]).astype(BF16)

    acc = jnp.dot(h_scr[...], w_ref[...].astype(BF16), preferred_element_type=F32)
    o_ref[...] = (acc * cs_ref[...]).astype(o_ref.dtype)


def _proj(x2, g, sc, sh, w_in, seq):
    t, d = x2.shape
    n = w_in.shape[1]
    tm = min(1024, seq)
    tn = 1024
    bpt = seq // tm
    col_scale = jnp.ones((1, n), F32).at[:, CB_AQ * LANES:CB_AK * LANES].set(Q_PRESCALE)
    return pl.pallas_call(
        _proj_kernel,
        grid=(t // tm, n // tn),
        in_specs=[pl.BlockSpec((tm, d), lambda i, j: (i, 0)),
                  pl.BlockSpec((1, d), lambda i, j: (0, 0)),
                  pl.BlockSpec((1, 1, d), lambda i, j: (i // bpt, 0, 0)),
                  pl.BlockSpec((1, 1, d), lambda i, j: (i // bpt, 0, 0)),
                  pl.BlockSpec((d, tn), lambda i, j: (0, j)),
                  pl.BlockSpec((1, tn), lambda i, j: (0, j))],
        out_specs=pl.BlockSpec((tm, tn), lambda i, j: (i, j)),
        out_shape=jax.ShapeDtypeStruct((t, n), BF16),
        scratch_shapes=[pltpu.VMEM((tm, d), BF16)],
        compiler_params=_cparams(("arbitrary", "arbitrary")),
        name="proj",
    )(x2, g, sc, sh, w_in, col_scale)


def _moba_kernel(q_ref, k_ref, v_ref, o_ref, kaug_scr, vt_scr, kmean_scr, s_scr, *, nb, gr):
    blk = MOBA_BLOCK
    gk = MOBA_GROUP * blk
    seq = k_ref.shape[0]

    k = k_ref[...]
    kaug_scr[:, :LANES] = k
    row_blk = lax.broadcasted_iota(jnp.int32, (seq, LANES), 0) // blk
    lane = lax.broadcasted_iota(jnp.int32, (seq, LANES), 1)
    kaug_scr[:, LANES:] = jnp.where(lane == row_blk, 1.0, 0.0).astype(BF16)
    kmean_scr[...] = jnp.zeros_like(kmean_scr)
    kmean_scr[0:nb, :] = jnp.mean(k.astype(F32).reshape(nb, blk, LANES), axis=1)
    for j in range(nb):
        vj = v_ref[j * blk:(j + 1) * blk, :].astype(F32)
        vt_scr[0:LANES, j * blk:(j + 1) * blk] = vj.T.astype(BF16)
    vt_scr[LANES:LANES + 16, :] = jnp.where(
        lax.broadcasted_iota(jnp.int32, (16, seq), 0) == 0, 1.0, 0.0).astype(BF16)

    rid = lax.broadcasted_iota(jnp.int32, (gr, blk), 0)
    for i in range(nb):
        q = q_ref[i * blk:(i + 1) * blk, :]
        gate = _dot_nt(kmean_scr[...], q.astype(F32), precision=HIGHEST)
        gate = jnp.where(rid < i, gate, -jnp.inf)
        sel = rid == i
        for _ in range(MOBA_TOPK):
            m = jnp.max(gate, axis=0, keepdims=True)
            idx = jnp.min(jnp.where(gate == m, rid, gr), axis=0, keepdims=True)
            pick = rid == idx
            sel = sel | (pick & (rid < i))
            gate = jnp.where(pick, -jnp.inf, gate)
        bias_t = jnp.where(sel, 0.0, NEG_BIG)
        bias_t = jnp.concatenate([bias_t, jnp.zeros((LANES - gr, blk), F32)], axis=0)
        q_aug = jnp.concatenate([q, bias_t.T.astype(BF16)], axis=-1)

        n = (i + 1) * blk
        spans = [(r0, min(r0 + gk, n)) for r0 in range(0, n, gk)]
        slot = i % 2
        mx = jnp.full((8, blk), NEG_BIG, F32)
        for r0, r1 in spans:
            s = _dot_nt(kaug_scr[r0:r1, :], q_aug)
            if r1 == n:
                kpos = r0 + lax.broadcasted_iota(jnp.int32, (r1 - r0, blk), 0)
                qpos = i * blk + lax.broadcasted_iota(jnp.int32, (r1 - r0, blk), 1)
                s = jnp.where(kpos <= qpos, s, NEG_BIG)
            s_scr[slot, r0:r1, :] = s
            mx = jnp.maximum(mx, jnp.max(s.reshape((r1 - r0) // 8, 8, blk), axis=0))
        m = jnp.max(mx, axis=0, keepdims=True)

        acc = jnp.zeros((LANES + 16, blk), F32)
        for r0, r1 in spans:
            p = jnp.exp2(s_scr[slot, r0:r1, :] - m).astype(BF16)
            acc = acc + jnp.dot(vt_scr[:, r0:r1], p, preferred_element_type=F32)
        out_t = acc[0:LANES] / acc[LANES:LANES + 1]
        o_ref[i * blk:(i + 1) * blk, :] = out_t.T.astype(o_ref.dtype)


def _moba(proj, batch, seq):
    nb = seq // MOBA_BLOCK
    blk = MOBA_BLOCK
    gr = -(-nb // 8) * 8
    t = batch * seq
    assert gr <= LANES
    return pl.pallas_call(
        functools.partial(_moba_kernel, nb=nb, gr=gr),
        grid=(batch, A_HEADS),
        in_specs=[pl.BlockSpec((seq, LANES), lambda b, h: (b, CB_AQ + h)),
                  pl.BlockSpec((seq, LANES), lambda b, h: (b, CB_AK + h)),
                  pl.BlockSpec((seq, LANES), lambda b, h: (b, CB_AV + h))],
        out_specs=pl.BlockSpec((seq, LANES), lambda b, h: (b, h)),
        out_shape=jax.ShapeDtypeStruct((t, A_WIDTH), BF16),
        scratch_shapes=[pltpu.VMEM((seq, 2 * LANES), BF16),
                        pltpu.VMEM((LANES + 16, seq), BF16),
                        pltpu.VMEM((gr, LANES), F32),
                        pltpu.VMEM((2, seq, blk), F32)],
        compiler_params=_cparams(("arbitrary", "arbitrary")),
        name="moba",
    )(proj, proj, proj)


def _hgrn_kernel(q_ref, f_ref, i_ref, og_ref, rl_ref, g_ref, o_ref, st_scr, b_scr, *, layer):
    tt = q_ref.shape[0]
    ch, sub = HGRN_CHUNK, HGRN_SUB
    ns, nc = ch // sub, tt // ch
    assert SUBLANES % sub == 0 and ch % SUBLANES == 0 and tt % ch == 0

    @pl.when(pl.program_id(2) == 0)
    def _():
        st_scr[...] = jnp.zeros_like(st_scr)

    rl = rl_ref[...]
    e = jnp.exp(rl - jnp.max(rl, axis=0, keepdims=True))
    lb = jnp.sum(e[: layer + 1], axis=0, keepdims=True) / jnp.sum(e, axis=0, keepdims=True)

    q = q_ref[...].astype(F32)
    x = f_ref[...].astype(F32)
    v_bf = i_ref[...]
    v = v_bf.astype(F32)
    f = lb + (1.0 - lb) * jax.nn.sigmoid(x)
    kin = (1.0 - lb) * jax.nn.sigmoid(-x)
    row = lax.broadcasted_iota(jnp.int32, (tt, LANES), 0)
    pic = row % ch
    pos = row % sub

    def roll8(a, k):
        return pltpu.roll(a.reshape(tt // SUBLANES, SUBLANES, LANES), k, 1).reshape(tt, LANES)

    b = jnp.log2(f)
    pos8 = row % SUBLANES
    step = 1
    while step < SUBLANES:
        b = b + jnp.where(pos8 >= step, roll8(b, step), 0.0)
        step *= 2
    gpc = ch // SUBLANES
    b4 = b.reshape(nc, gpc, SUBLANES, LANES)
    offs = [jnp.zeros((nc, 1, 1, LANES), F32)]
    for g in range(1, gpc):
        offs.append(offs[-1] + b4[:, g - 1:g, SUBLANES - 1:SUBLANES, :])
    b = (b4 + jnp.concatenate(offs, axis=1)).reshape(tt, LANES)
    b_scr[...] = b

    def chunk_rows(offset):
        return b_scr[pl.ds(offset, nc, stride=ch), :]

    def spread(r):
        return jnp.broadcast_to(r[:, None, :], (nc, ch, LANES)).reshape(tt, LANES)

    r_sub = b
    for k in range(1, sub):
        r_sub = jnp.where(pos == sub - 1 - k, roll8(b, SUBLANES - k), r_sub)
    r_last_c = chunk_rows(ch - 1)
    r_last = spread(r_last_c)

    kh = kin * jnp.exp2(r_sub - b)
    sid = pic // sub

    q_blocks, k_blocks = [], []
    for j in range(ns - 1):
        rj = spread(chunk_rows(sub * (j + 1) - 1))
        qj = q * jnp.exp2(jnp.where(pic >= sub * (j + 1), b - rj, -jnp.inf))
        q_blocks.append(qj.astype(BF16))
        k_blocks.append(jnp.where(sid == j, kh, 0.0).astype(BF16))
    qe = (q * jnp.exp2(b)).astype(BF16)
    ke = (kh * jnp.exp2(r_last - r_sub)).astype(BF16)
    dec = jnp.exp2(r_last_c)

    diag = jnp.sum(q * kin, axis=-1, keepdims=True) * v
    for delta in range(1, sub):
        kr = roll8(kin, delta)
        br = roll8(b, delta)
        vr = roll8(v, delta)
        w = jnp.where(pos >= delta, q * kr * jnp.exp2(b - br), 0.0)
        diag = diag + jnp.sum(w, axis=-1, keepdims=True) * vr

    st = st_scr[...]
    outs = []
    for c in range(nc):
        sl = slice(c * ch, (c + 1) * ch)
        qc = jnp.concatenate([qb[sl] for qb in q_blocks], axis=-1)
        kc = jnp.concatenate([kb[sl] for kb in k_blocks], axis=-1)
        a_off = _dot_nt(qc, kc)
        vc = v_bf[sl]
        o_c = jnp.dot(a_off.astype(BF16), vc, preferred_element_type=F32)
        o_c = o_c + _dot_nt(qe[sl], st.astype(BF16)) + diag[sl]
        st = dec[c:c + 1] * st + _dot_tn(vc, ke[sl])
        outs.append(o_c)
    st_scr[...] = st
    o = jnp.concatenate(outs, axis=0)

    y = o * lax.rsqrt(jnp.mean(o * o, axis=-1, keepdims=True) + EPS) * g_ref[...]
    og = og_ref[...].astype(F32)
    o_ref[...] = (y * (og * jax.nn.sigmoid(og))).astype(o_ref.dtype)


def _hgrn(proj, r_lower, r_norm_g, batch, seq, layer):
    tt = min(HGRN_TILE, seq)
    nt = seq // tt
    t = batch * seq
    nl = r_lower.shape[0]

    def col(cb):
        return pl.BlockSpec((tt, LANES), lambda b, h, c: (b * nt + c, cb + h))

    return pl.pallas_call(
        functools.partial(_hgrn_kernel, layer=layer),
        grid=(batch, R_HEADS, nt),
        in_specs=[col(CB_RQ), col(CB_RF), col(CB_RI), col(CB_ROG),
                  pl.BlockSpec((nl, LANES), lambda b, h, c: (0, h)),
                  pl.BlockSpec((1, LANES), lambda b, h, c: (0, 0))],
        out_specs=pl.BlockSpec((tt, LANES), lambda b, h, c: (b * nt + c, h)),
        out_shape=jax.ShapeDtypeStruct((t, R_WIDTH), BF16),
        scratch_shapes=[pltpu.VMEM((LANES, LANES), F32), pltpu.VMEM((tt, LANES), F32)],
        compiler_params=_cparams(("arbitrary", "arbitrary", "arbitrary")),
        name="hgrn",
    )(proj, proj, proj, proj, r_lower, r_norm_g)


def _merge_kernel(att_ref, or_ref, ga0_ref, ga1_ref, gr0_ref, gr1_ref, x_ref, gt_ref,
                  wua_ref, wur_ref, wo_ref, g2_ref, sc_ref, sh_ref, wr_ref, br_ref,
                  x1_ref, h2_ref, lg_ref):
    ya = jnp.dot(att_ref[...], wua_ref[...], preferred_element_type=F32)
    yr = jnp.dot(or_ref[...], wur_ref[...], preferred_element_type=F32)
    ga = jnp.concatenate([ga0_ref[...], ga1_ref[...]], axis=-1).astype(F32)
    gr = jnp.concatenate([gr0_ref[...], gr1_ref[...]], axis=-1).astype(F32)
    merged = jax.nn.sigmoid(ga) * ya + jax.nn.sigmoid(gr) * yr
    out = jnp.dot(merged.astype(BF16), wo_ref[...], preferred_element_type=F32)
    x1 = x_ref[...] + gt_ref[0] * out
    x1_ref[...] = x1
    ms = jnp.mean(x1 * x1, axis=-1, keepdims=True)
    h2 = (x1 * lax.rsqrt(ms + EPS) * g2_ref[...]) * (1.0 + sc_ref[0]) + sh_ref[0]
    h2_ref[...] = h2
    lg_ref[...] = jnp.dot(h2, wr_ref[...], preferred_element_type=F32, precision=HIGHEST) + br_ref[...]


def _merge(att, o_r, proj, x2, gt1, w_up_a, w_up_r, w_out, g2, sc2, sh2, w_router, b_router, seq):
    t, d = x2.shape
    tm = 256
    bpt = seq // tm
    half = d // 2
    cb = half // LANES

    def gspec(cb0, k):
        return pl.BlockSpec((tm, half), lambda i: (i, cb0 // cb + k))

    def const(shape):
        return pl.BlockSpec(shape, lambda i: tuple(0 for _ in shape), pipeline_mode=pl.Buffered(1))

    def per_batch():
        return pl.BlockSpec((1, 1, d), lambda i: (i // bpt, 0, 0))

    return pl.pallas_call(
        _merge_kernel,
        grid=(t // tm,),
        in_specs=[pl.BlockSpec((tm, A_WIDTH), lambda i: (i, 0)),
                  pl.BlockSpec((tm, R_WIDTH), lambda i: (i, 0)),
                  gspec(CB_GA, 0), gspec(CB_GA, 1), gspec(CB_GR, 0), gspec(CB_GR, 1),
                  pl.BlockSpec((tm, d), lambda i: (i, 0)),
                  per_batch(),
                  const((A_WIDTH, d)), const((R_WIDTH, d)), const((d, d)),
                  const((1, d)), per_batch(), per_batch(),
                  const((d, LANES)), const((1, LANES))],
        out_specs=[pl.BlockSpec((tm, d), lambda i: (i, 0)),
                   pl.BlockSpec((tm, d), lambda i: (i, 0)),
                   pl.BlockSpec((tm, LANES), lambda i: (i, 0))],
        out_shape=[jax.ShapeDtypeStruct((t, d), F32),
                   jax.ShapeDtypeStruct((t, d), F32),
                   jax.ShapeDtypeStruct((t, LANES), F32)],
        compiler_params=_cparams(("arbitrary",)),
        name="merge",
    )(att, o_r, proj, proj, proj, proj, x2, gt1, w_up_a, w_up_r, w_out, g2, sc2, sh2, w_router, b_router)


RT_E0, RT_E1, RT_W0, RT_W1, RT_R0, RT_R1 = 0, 1, 2, 3, 4, 5


def _route_kernel(lg_ref, rt_ref, cnt_ref, carry_scr):
    tr = lg_ref.shape[0]

    @pl.when(pl.program_id(0) == 0)
    def _():
        carry_scr[...] = jnp.zeros_like(carry_scr)

    x = lg_ref[...]
    lane = lax.broadcasted_iota(jnp.int32, (tr, LANES), 1)
    ninf = -jnp.inf

    def lane_max(val):
        return jnp.max(val, axis=-1, keepdims=True)

    def first_lane(mask):
        return jnp.min(jnp.where(mask, lane, LANES), axis=-1, keepdims=True)

    is_g = lane < N_GROUPS
    gmax = lane_max(jnp.where(is_g, x, ninf))
    grp = first_lane(is_g & (x == gmax))
    eg = jnp.where(is_g, jnp.exp(x - gmax), 0.0)
    pg_top = 1.0 / jnp.sum(eg, axis=-1, keepdims=True)

    lo = N_GROUPS + grp * EXPERTS_PER_GROUP
    is_e = (lane >= lo) & (lane < lo + EXPERTS_PER_GROUP)
    emax = lane_max(jnp.where(is_e, x, ninf))
    ee = jnp.where(is_e, jnp.exp(x - emax), 0.0)
    pe = ee / jnp.sum(ee, axis=-1, keepdims=True)
    pe = jnp.where(is_e, pe, ninf)
    p0 = lane_max(pe)
    l0 = first_lane(pe == p0)
    pe1 = jnp.where(lane == l0, ninf, pe)
    p1 = lane_max(pe1)
    l1 = first_lane(pe1 == p1)
    den = p0 + p1
    w0 = pg_top * p0 / den
    w1 = pg_top * p1 / den
    e0 = l0 - N_GROUPS
    e1 = l1 - N_GROUPS

    onehot = ((lane == e0) | (lane == e1)).astype(BF16)
    r = lax.broadcasted_iota(jnp.int32, (tr, tr), 0)
    c = lax.broadcasted_iota(jnp.int32, (tr, tr), 1)
    tri = jnp.where(c < r, 1.0, 0.0).astype(BF16)
    prefix = jnp.dot(tri, onehot, preferred_element_type=F32) + carry_scr[0:1, :]
    rank0 = jnp.sum(jnp.where(lane == e0, prefix, 0.0), axis=-1, keepdims=True)
    rank1 = jnp.sum(jnp.where(lane == e1, prefix, 0.0), axis=-1, keepdims=True)
    total = carry_scr[0:1, :] + jnp.sum(onehot.astype(F32), axis=0, keepdims=True)
    carry_scr[...] = jnp.broadcast_to(total, carry_scr.shape)
    cnt_ref[...] = jnp.broadcast_to(total, cnt_ref.shape)

    rec = jnp.zeros((tr, LANES), F32)
    for k, val in ((RT_E0, e0.astype(F32)), (RT_E1, e1.astype(F32)), (RT_W0, w0), (RT_W1, w1),
                   (RT_R0, rank0), (RT_R1, rank1)):
        rec = jnp.where(lane == k, val, rec)
    rt_ref[...] = rec


def _route(logits):
    t = logits.shape[0]
    tr = 512
    return pl.pallas_call(
        _route_kernel,
        grid=(t // tr,),
        in_specs=[pl.BlockSpec((tr, LANES), lambda i: (i, 0))],
        out_specs=[pl.BlockSpec((tr, LANES), lambda i: (i, 0)),
                   pl.BlockSpec((8, LANES), lambda i: (0, 0))],
        out_shape=[jax.ShapeDtypeStruct((t, LANES), F32),
                   jax.ShapeDtypeStruct((8, LANES), F32)],
        scratch_shapes=[pltpu.VMEM((8, LANES), F32)],
        compiler_params=_cparams(("arbitrary",)),
        name="route",
    )(logits)


def _row_copy(src_hbm, dst_buf, slot, src_row, dst_row, sem):
    return pltpu.make_async_copy(src_hbm.at[pl.ds(src_row, 1), :],
                                 dst_buf.at[slot, pl.ds(dst_row, 1), :],
                                 sem.at[slot])


def _moe_kernel(tok_ref, be_ref, nu_ref, h_hbm, w1_ref, w3_ref, w2_ref, y_ref, xbuf, sem, w1_bf, w3_bf, w2_bf):
    rows = MOE_BLOCK
    i = pl.program_id(0)
    nused = nu_ref[0]

    def gather(block, slot, start):
        base = block * rows

        def body(r, _):
            cp = _row_copy(h_hbm, xbuf, slot, tok_ref[base + r], r, sem)
            if start:
                cp.start()
            else:
                cp.wait()
            return 0

        lax.fori_loop(0, rows, body, 0, unroll=8)

    @pl.when((i == 0) & (nused > 0))
    def _():
        gather(0, 0, True)

    @pl.when(i + 1 < nused)
    def _():
        gather(i + 1, (i + 1) % 2, True)

    new_expert = (i == 0) | (be_ref[i] != be_ref[jnp.maximum(i - 1, 0)])

    @pl.when((i < nused) & new_expert)
    def _():
        w1_bf[...] = w1_ref[0].astype(BF16)
        w3_bf[...] = w3_ref[0].astype(BF16)
        w2_bf[...] = w2_ref[0].astype(BF16)

    @pl.when(i < nused)
    def _():
        slot = i % 2
        gather(i, slot, False)
        xb = xbuf[slot].astype(BF16)
        h1 = jnp.dot(xb, w1_bf[...], preferred_element_type=F32)
        h3 = jnp.dot(xb, w3_bf[...], preferred_element_type=F32)
        hid = (h1 * jax.nn.sigmoid(h1)) * h3
        y_ref[...] = jnp.dot(hid.astype(BF16), w2_bf[...], preferred_element_type=F32)

    @pl.when(i >= nused)
    def _():
        y_ref[...] = jnp.zeros_like(y_ref)


def _moe(row_tok, blk_e, nused, h2, w1, w3, w2):
    t, d = h2.shape
    r = row_tok.shape[0]
    nblk = r // MOE_BLOCK
    grid_spec = pltpu.PrefetchScalarGridSpec(
        num_scalar_prefetch=3,
        grid=(nblk,),
        in_specs=[pl.BlockSpec(memory_space=pl.ANY),
                  pl.BlockSpec((1, d, D_EXPERT), lambda i, tok, be, nu: (be[i], 0, 0)),
                  pl.BlockSpec((1, d, D_EXPERT), lambda i, tok, be, nu: (be[i], 0, 0)),
                  pl.BlockSpec((1, D_EXPERT, d), lambda i, tok, be, nu: (be[i], 0, 0))],
        out_specs=pl.BlockSpec((MOE_BLOCK, d), lambda i, tok, be, nu: (i, 0)),
        scratch_shapes=[pltpu.VMEM((2, MOE_BLOCK, d), F32), pltpu.SemaphoreType.DMA((2,)),
                        pltpu.VMEM((d, D_EXPERT), BF16), pltpu.VMEM((d, D_EXPERT), BF16),
                        pltpu.VMEM((D_EXPERT, d), BF16)],
    )
    return pl.pallas_call(
        _moe_kernel,
        grid_spec=grid_spec,
        out_shape=jax.ShapeDtypeStruct((r, d), F32),
        compiler_params=_cparams(("arbitrary",)),
        name="moe",
    )(row_tok, blk_e, nused, h2, w1, w3, w2)


def _final_kernel(dest_ref, y_hbm, x1_ref, rt_ref, gt_ref, g_ref, o_ref, ybuf, sem, *, last_layer):
    tf = x1_ref.shape[0]
    i = pl.program_id(0)
    n = pl.num_programs(0)

    def gather(step, slot, start):
        base = step * tf * TOP_K_INNER

        def body(r, _):
            for k in range(TOP_K_INNER):
                cp = pltpu.make_async_copy(
                    y_hbm.at[pl.ds(dest_ref[base + r * TOP_K_INNER + k], 1), :],
                    ybuf.at[slot, k, pl.ds(r, 1), :],
                    sem.at[slot])
                if start:
                    cp.start()
                else:
                    cp.wait()
            return 0

        lax.fori_loop(0, tf, body, 0, unroll=8)

    @pl.when(i == 0)
    def _():
        gather(0, 0, True)

    @pl.when(i + 1 < n)
    def _():
        gather(i + 1, (i + 1) % 2, True)

    slot = i % 2
    gather(i, slot, False)
    rt = rt_ref[...]
    w0 = rt[:, RT_W0:RT_W0 + 1]
    w1 = rt[:, RT_W1:RT_W1 + 1]
    y = ybuf[slot, 0] * w0 + ybuf[slot, 1] * w1
    x2 = x1_ref[...] + gt_ref[0] * y
    if last_layer:
        ms = jnp.mean(x2 * x2, axis=-1, keepdims=True)
        x2 = x2 * lax.rsqrt(ms + EPS) * g_ref[...]
    o_ref[...] = x2


def _final(dest, yr, x1, rt, gt2, final_g, seq, last_layer):
    t, d = x1.shape
    tf = 256
    bpt = seq // tf
    grid_spec = pltpu.PrefetchScalarGridSpec(
        num_scalar_prefetch=1,
        grid=(t // tf,),
        in_specs=[pl.BlockSpec(memory_space=pl.ANY),
                  pl.BlockSpec((tf, d), lambda i, dst: (i, 0)),
                  pl.BlockSpec((tf, LANES), lambda i, dst: (i, 0)),
                  pl.BlockSpec((1, 1, d), lambda i, dst: (i // bpt, 0, 0)),
                  pl.BlockSpec((1, d), lambda i, dst: (0, 0))],
        out_specs=pl.BlockSpec((tf, d), lambda i, dst: (i, 0)),
        scratch_shapes=[pltpu.VMEM((2, TOP_K_INNER, tf, d), F32), pltpu.SemaphoreType.DMA((2,))],
    )
    return pl.pallas_call(
        functools.partial(_final_kernel, last_layer=last_layer),
        grid_spec=grid_spec,
        out_shape=jax.ShapeDtypeStruct((t, d), F32),
        compiler_params=_cparams(("arbitrary",)),
        name="final",
    )(dest, yr, x1, rt, gt2, final_g)


def _dispatch_tables(rt, counts_row):
    t = rt.shape[0]
    tk = t * TOP_K_INNER
    eid = rt[:, RT_E0:RT_E1 + 1].astype(jnp.int32)
    rank = rt[:, RT_R0:RT_R1 + 1].astype(jnp.int32)
    counts = counts_row[:N_EXPERTS].astype(jnp.int32)
    padded = (counts + MOE_BLOCK - 1) // MOE_BLOCK * MOE_BLOCK
    pad_end = jnp.cumsum(padded)
    pad_start = pad_end - padded
    dest = (pad_start[eid] + rank).reshape(tk)
    r = tk + N_EXPERTS * MOE_BLOCK
    tok = jnp.repeat(jnp.arange(t, dtype=jnp.int32), TOP_K_INNER)
    row_tok = jnp.zeros((r,), jnp.int32).at[dest].set(tok)
    nblk = r // MOE_BLOCK
    blk_start = jnp.arange(nblk, dtype=jnp.int32) * MOE_BLOCK
    blk_e = jnp.minimum(jnp.sum(pad_end[None, :] <= blk_start[:, None], axis=1), N_EXPERTS - 1).astype(jnp.int32)
    nused = (pad_end[-1:] // MOE_BLOCK).astype(jnp.int32)
    return dest.astype(jnp.int32), row_tok, blk_e, nused


def kernel(x, c, norm1_g, norm2_g, final_g, w_ada, b_ada, w_in, r_lower, r_norm_g,
           w_up_a, w_up_r, w_out, w_rg, b_rg, w_re, b_re, w1, w3, w2):
    batch, seq, d = x.shape
    t = batch * seq
    depth = w_in.shape[0]
    x2 = x.reshape(t, d)
    c_pad = jnp.zeros((8, d), F32).at[:batch].set(c)
    for l in range(depth):
        mod = _ada(c_pad, w_ada[l], b_ada[l][None, :])[:batch]
        sh1, sc1, gt1, sh2, sc2, gt2 = [m[:, None, :] for m in jnp.split(mod, 6, axis=-1)]
        proj = _proj(x2, norm1_g[l][None, :], sc1, sh1, w_in[l], seq)
        att = _moba(proj, batch, seq)
        o_r = _hgrn(proj, r_lower, r_norm_g[l][None, :], batch, seq, l)
        w_router = jnp.zeros((d, LANES), F32).at[:, :N_GROUPS].set(w_rg[l]) \
            .at[:, N_GROUPS:N_GROUPS + N_EXPERTS].set(w_re[l])
        b_router = jnp.zeros((1, LANES), F32).at[0, :N_GROUPS].set(b_rg[l]) \
            .at[0, N_GROUPS:N_GROUPS + N_EXPERTS].set(b_re[l])
        x1, h2, logits = _merge(att, o_r, proj, x2, gt1, w_up_a[l].astype(BF16), w_up_r[l].astype(BF16),
                                w_out[l].astype(BF16), norm2_g[l][None, :], sc2, sh2, w_router, b_router, seq)
        rt, cnt = _route(logits)
        dest, row_tok, blk_e, nused = _dispatch_tables(rt, cnt[0])
        yr = _moe(row_tok, blk_e, nused, h2, w1[l], w3[l], w2[l])
        x2 = _final(dest, yr, x1, rt, gt2, final_g[None, :], seq, l == depth - 1)
    return x2.reshape(batch, seq, d)
```

```python
import functools

import jax
import jax.numpy as jnp
from jax import lax
from jax.experimental import pallas as pl
from jax.experimental.pallas import tpu as pltpu

F32 = jnp.float32
BF16 = jnp.bfloat16
HIGHEST = lax.Precision.HIGHEST

D_MODEL = 2048
A_HEADS = 8
A_HEAD_DIM = 128
A_WIDTH = A_HEADS * A_HEAD_DIM
MOBA_BLOCK = 256
MOBA_TOPK = 3
R_HEADS = 8
R_KEY_DIM = 128
R_VAL_DIM = 128
R_WIDTH = R_HEADS * R_KEY_DIM
N_GROUPS = 4
EXPERTS_PER_GROUP = 8
N_EXPERTS = N_GROUPS * EXPERTS_PER_GROUP
TOP_K_INNER = 2
D_EXPERT = 512
MOE_BLOCK = 256
EPS = 1e-6
IN_COLS = 3 * A_WIDTH + 4 * R_WIDTH + 2 * D_MODEL

LANES = 128
SUBLANES = 8
CB_AQ = 0
CB_AK = CB_AQ + A_WIDTH // LANES
CB_AV = CB_AK + A_WIDTH // LANES
CB_RQ = CB_AV + A_WIDTH // LANES
CB_RF = CB_RQ + R_WIDTH // LANES
CB_RI = CB_RF + R_WIDTH // LANES
CB_ROG = CB_RI + R_WIDTH // LANES
CB_GA = CB_ROG + R_WIDTH // LANES
CB_GR = CB_GA + D_MODEL // LANES

MOBA_GROUP = 4
Q_PRESCALE = A_HEAD_DIM ** -0.5 * 1.4426950408889634
HGRN_TILE = 512
HGRN_CHUNK = 32
HGRN_SUB = 4
NEG_BIG = -1e30
VMEM_LIMIT = 56 * 1024 * 1024


def _cparams(sem):
    return pltpu.CompilerParams(dimension_semantics=sem, vmem_limit_bytes=VMEM_LIMIT)


def _dot_nt(a, b, **kw):
    return lax.dot_general(a, b, (((1,), (1,)), ((), ())), preferred_element_type=F32, **kw)


def _bf16_part(a):
    bits = lax.bitcast_convert_type(a, jnp.uint32) & jnp.uint32(0xFFFF0000)
    return lax.bitcast_convert_type(bits, F32)


def _dot_tn(a, b):
    return lax.dot_general(a, b, (((0,), (0,)), ((), ())), preferred_element_type=F32)


def _ada_kernel(c_ref, w_ref, b_ref, o_ref):
    c = c_ref[...]
    ca = c * jax.nn.sigmoid(c)
    o_ref[...] = jnp.dot(ca, w_ref[...], preferred_element_type=F32, precision=HIGHEST) + b_ref[...]


def _ada(c_pad, w_ada, b_ada):
    rows, d = c_pad.shape
    n = w_ada.shape[1]
    tn = 1024
    return pl.pallas_call(
        _ada_kernel,
        grid=(n // tn,),
        in_specs=[pl.BlockSpec((rows, d), lambda j: (0, 0)),
                  pl.BlockSpec((d, tn), lambda j: (0, j)),
                  pl.BlockSpec((1, tn), lambda j: (0, j))],
        out_specs=pl.BlockSpec((rows, tn), lambda j: (0, j)),
        out_shape=jax.ShapeDtypeStruct((rows, n), F32),
        compiler_params=_cparams(("arbitrary",)),
        name="ada",
    )(c_pad, w_ada, b_ada)


def _proj_kernel(x_ref, g_ref, sc_ref, sh_ref, w_ref, cs_ref, o_ref, h_scr):
    @pl.when(pl.program_id(1) == 0)
    def _():
        x = x_ref[...]
        ms = jnp.mean(x * x, axis=-1, keepdims=True)
        y = x * lax.rsqrt(ms + EPS) * g_ref[...]
        h_scr[...] = (y * (1.0 + sc_ref[0]) + sh_ref[0]).astype(BF16)

    acc = jnp.dot(h_scr[...], w_ref[...].astype(BF16), preferred_element_type=F32)
    o_ref[...] = (acc * cs_ref[...]).astype(o_ref.dtype)


def _proj(x2, g, sc, sh, w_in, seq):
    t, d = x2.shape
    n = w_in.shape[1]
    tm = min(1024, seq)
    tn = 1024
    bpt = seq // tm
    col_scale = jnp.ones((1, n), F32).at[:, CB_AQ * LANES:CB_AK * LANES].set(Q_PRESCALE)
    return pl.pallas_call(
        _proj_kernel,
        grid=(t // tm, n // tn),
        in_specs=[pl.BlockSpec((tm, d), lambda i, j: (i, 0)),
                  pl.BlockSpec((1, d), lambda i, j: (0, 0)),
                  pl.BlockSpec((1, 1, d), lambda i, j: (i // bpt, 0, 0)),
                  pl.BlockSpec((1, 1, d), lambda i, j: (i // bpt, 0, 0)),
                  pl.BlockSpec((d, tn), lambda i, j: (0, j)),
                  pl.BlockSpec((1, tn), lambda i, j: (0, j))],
        out_specs=pl.BlockSpec((tm, tn), lambda i, j: (i, j)),
        out_shape=jax.ShapeDtypeStruct((t, n), BF16),
        scratch_shapes=[pltpu.VMEM((tm, d), BF16)],
        compiler_params=_cparams(("arbitrary", "arbitrary")),
        name="proj",
    )(x2, g, sc, sh, w_in, col_scale)


def _moba_kernel(q_ref, k_ref, v_ref, o_ref, kaug_scr, vt_scr, kmean_scr, s_scr, *, nb, gr):
    blk = MOBA_BLOCK
    gk = MOBA_GROUP * blk
    seq = k_ref.shape[0]

    k = k_ref[...]
    kaug_scr[:, :LANES] = k
    row_blk = lax.broadcasted_iota(jnp.int32, (seq, LANES), 0) // blk
    lane = lax.broadcasted_iota(jnp.int32, (seq, LANES), 1)
    kaug_scr[:, LANES:] = jnp.where(lane == row_blk, 1.0, 0.0).astype(BF16)
    kmean_scr[...] = jnp.zeros_like(kmean_scr)
    kmean_scr[0:nb, :] = jnp.mean(k.astype(F32).reshape(nb, blk, LANES), axis=1)
    for j in range(nb):
        vj = v_ref[j * blk:(j + 1) * blk, :].astype(F32)
        vt_scr[0:LANES, j * blk:(j + 1) * blk] = vj.T.astype(BF16)
    vt_scr[LANES:LANES + 16, :] = jnp.where(
        lax.broadcasted_iota(jnp.int32, (16, seq), 0) == 0, 1.0, 0.0).astype(BF16)

    rid = lax.broadcasted_iota(jnp.int32, (gr, blk), 0)
    for i in range(nb):
        q = q_ref[i * blk:(i + 1) * blk, :]
        gate = _dot_nt(kmean_scr[...], q.astype(F32), precision=HIGHEST)
        gate = jnp.where(rid < i, gate, -jnp.inf)
        sel = rid == i
        for _ in range(MOBA_TOPK):
            m = jnp.max(gate, axis=0, keepdims=True)
            idx = jnp.min(jnp.where(gate == m, rid, gr), axis=0, keepdims=True)
            pick = rid == idx
            sel = sel | (pick & (rid < i))
            gate = jnp.where(pick, -jnp.inf, gate)
        bias_t = jnp.where(sel, 0.0, NEG_BIG)
        bias_t = jnp.concatenate([bias_t, jnp.zeros((LANES - gr, blk), F32)], axis=0)
        q_aug = jnp.concatenate([q, bias_t.T.astype(BF16)], axis=-1)

        n = (i + 1) * blk
        spans = [(r0, min(r0 + gk, n)) for r0 in range(0, n, gk)]
        slot = i % 2
        mx = jnp.full((8, blk), NEG_BIG, F32)
        for r0, r1 in spans:
            s = _dot_nt(kaug_scr[r0:r1, :], q_aug)
            if r1 == n:
                kpos = r0 + lax.broadcasted_iota(jnp.int32, (r1 - r0, blk), 0)
                qpos = i * blk + lax.broadcasted_iota(jnp.int32, (r1 - r0, blk), 1)
                s = jnp.where(kpos <= qpos, s, NEG_BIG)
            s_scr[slot, r0:r1, :] = s
            mx = jnp.maximum(mx, jnp.max(s.reshape((r1 - r0) // 8, 8, blk), axis=0))
        m = jnp.max(mx, axis=0, keepdims=True)

        acc = jnp.zeros((LANES + 16, blk), F32)
        for r0, r1 in spans:
            p = jnp.exp2(s_scr[slot, r0:r1, :] - m).astype(BF16)
            acc = acc + jnp.dot(vt_scr[:, r0:r1], p, preferred_element_type=F32)
        out_t = acc[0:LANES] / acc[LANES:LANES + 1]
        o_ref[i * blk:(i + 1) * blk, :] = out_t.T.astype(o_ref.dtype)


def _moba(proj, batch, seq):
    nb = seq // MOBA_BLOCK
    blk = MOBA_BLOCK
    gr = -(-nb // 8) * 8
    t = batch * seq
    assert gr <= LANES
    return pl.pallas_call(
        functools.partial(_moba_kernel, nb=nb, gr=gr),
        grid=(batch, A_HEADS),
        in_specs=[pl.BlockSpec((seq, LANES), lambda b, h: (b, CB_AQ + h)),
                  pl.BlockSpec((seq, LANES), lambda b, h: (b, CB_AK + h)),
                  pl.BlockSpec((seq, LANES), lambda b, h: (b, CB_AV + h))],
        out_specs=pl.BlockSpec((seq, LANES), lambda b, h: (b, h)),
        out_shape=jax.ShapeDtypeStruct((t, A_WIDTH), BF16),
        scratch_shapes=[pltpu.VMEM((seq, 2 * LANES), BF16),
                        pltpu.VMEM((LANES + 16, seq), BF16),
                        pltpu.VMEM((gr, LANES), F32),
                        pltpu.VMEM((2, seq, blk), F32)],
        compiler_params=_cparams(("arbitrary", "arbitrary")),
        name="moba",
    )(proj, proj, proj)


def _hgrn_kernel(q_ref, f_ref, i_ref, og_ref, rl_ref, g_ref, o_ref, st_scr, b_scr, *, layer):
    tt = q_ref.shape[0]
    ch, sub = HGRN_CHUNK, HGRN_SUB
    ns, nc = ch // sub, tt // ch
    assert SUBLANES % sub == 0 and ch % SUBLANES == 0 and tt % ch == 0

    @pl.when(pl.program_id(2) == 0)
    def _():
        st_scr[...] = jnp.zeros_like(st_scr)

    rl = rl_ref[...]
    e = jnp.exp(rl - jnp.max(rl, axis=0, keepdims=True))
    lb = jnp.sum(e[: layer + 1], axis=0, keepdims=True) / jnp.sum(e, axis=0, keepdims=True)

    q = q_ref[...].astype(F32)
    x = f_ref[...].astype(F32)
    v_bf = i_ref[...]
    v = v_bf.astype(F32)
    f = lb + (1.0 - lb) * jax.nn.sigmoid(x)
    kin = (1.0 - lb) * jax.nn.sigmoid(-x)
    row = lax.broadcasted_iota(jnp.int32, (tt, LANES), 0)
    pic = row % ch
    pos = row % sub

    def roll8(a, k):
        return pltpu.roll(a.reshape(tt // SUBLANES, SUBLANES, LANES), k, 1).reshape(tt, LANES)

    b = jnp.log2(f)
    pos8 = row % SUBLANES
    step = 1
    while step < SUBLANES:
        b = b + jnp.where(pos8 >= step, roll8(b, step), 0.0)
        step *= 2
    gpc = ch // SUBLANES
    b4 = b.reshape(nc, gpc, SUBLANES, LANES)
    offs = [jnp.zeros((nc, 1, 1, LANES), F32)]
    for g in range(1, gpc):
        offs.append(offs[-1] + b4[:, g - 1:g, SUBLANES - 1:SUBLANES, :])
    b = (b4 + jnp.concatenate(offs, axis=1)).reshape(tt, LANES)
    b_scr[...] = b

    def chunk_rows(offset):
        return b_scr[pl.ds(offset, nc, stride=ch), :]

    def spread(r):
        return jnp.broadcast_to(r[:, None, :], (nc, ch, LANES)).reshape(tt, LANES)

    r_sub = b
    for k in range(1, sub):
        r_sub = jnp.where(pos == sub - 1 - k, roll8(b, SUBLANES - k), r_sub)
    r_last_c = chunk_rows(ch - 1)
    r_last = spread(r_last_c)

    kh = kin * jnp.exp2(r_sub - b)
    sid = pic // sub

    q_blocks, k_blocks = [], []
    for j in range(ns - 1):
        rj = spread(chunk_rows(sub * (j + 1) - 1))
        qj = q * jnp.exp2(jnp.where(pic >= sub * (j + 1), b - rj, -jnp.inf))
        q_blocks.append(qj.astype(BF16))
        k_blocks.append(jnp.where(sid == j, kh, 0.0).astype(BF16))
    qe = (q * jnp.exp2(b)).astype(BF16)
    ke = (kh * jnp.exp2(r_last - r_sub)).astype(BF16)
    dec = jnp.exp2(r_last_c)

    diag = jnp.sum(q * kin, axis=-1, keepdims=True) * v
    for delta in range(1, sub):
        kr = roll8(kin, delta)
        br = roll8(b, delta)
        vr = roll8(v, delta)
        w = jnp.where(pos >= delta, q * kr * jnp.exp2(b - br), 0.0)
        diag = diag + jnp.sum(w, axis=-1, keepdims=True) * vr

    st = st_scr[...]
    outs = []
    for c in range(nc):
        sl = slice(c * ch, (c + 1) * ch)
        qc = jnp.concatenate([qb[sl] for qb in q_blocks], axis=-1)
        kc = jnp.concatenate([kb[sl] for kb in k_blocks], axis=-1)
        a_off = _dot_nt(qc, kc)
        vc = v_bf[sl]
        o_c = jnp.dot(a_off.astype(BF16), vc, preferred_element_type=F32)
        o_c = o_c + _dot_nt(qe[sl], st.astype(BF16)) + diag[sl]
        st = dec[c:c + 1] * st + _dot_tn(vc, ke[sl])
        outs.append(o_c)
    st_scr[...] = st
    o = jnp.concatenate(outs, axis=0)

    y = o * lax.rsqrt(jnp.mean(o * o, axis=-1, keepdims=True) + EPS) * g_ref[...]
    og = og_ref[...].astype(F32)
    o_ref[...] = (y * (og * jax.nn.sigmoid(og))).astype(o_ref.dtype)


def _hgrn(proj, r_lower, r_norm_g, batch, seq, layer):
    tt = min(HGRN_TILE, seq)
    nt = seq // tt
    t = batch * seq
    nl = r_lower.shape[0]

    def col(cb):
        return pl.BlockSpec((tt, LANES), lambda b, h, c: (b * nt + c, cb + h))

    return pl.pallas_call(
        functools.partial(_hgrn_kernel, layer=layer),
        grid=(batch, R_HEADS, nt),
        in_specs=[col(CB_RQ), col(CB_RF), col(CB_RI), col(CB_ROG),
                  pl.BlockSpec((nl, LANES), lambda b, h, c: (0, h)),
                  pl.BlockSpec((1, LANES), lambda b, h, c: (0, 0))],
        out_specs=pl.BlockSpec((tt, LANES), lambda b, h, c: (b * nt + c, h)),
        out_shape=jax.ShapeDtypeStruct((t, R_WIDTH), BF16),
        scratch_shapes=[pltpu.VMEM((LANES, LANES), F32), pltpu.VMEM((tt, LANES), F32)],
        compiler_params=_cparams(("arbitrary", "arbitrary", "arbitrary")),
        name="hgrn",
    )(proj, proj, proj, proj, r_lower, r_norm_g)


def _merge_kernel(att_ref, or_ref, ga0_ref, ga1_ref, gr0_ref, gr1_ref, x_ref, gt_ref,
                  wua_ref, wur_ref, wo_ref, g2_ref, sc_ref, sh_ref, wr_ref, br_ref,
                  x1_ref, h2_ref, lg_ref):
    ya = jnp.dot(att_ref[...], wua_ref[...], preferred_element_type=F32)
    yr = jnp.dot(or_ref[...], wur_ref[...], preferred_element_type=F32)
    ga = jnp.concatenate([ga0_ref[...], ga1_ref[...]], axis=-1).astype(F32)
    gr = jnp.concatenate([gr0_ref[...], gr1_ref[...]], axis=-1).astype(F32)
    merged = jax.nn.sigmoid(ga) * ya + jax.nn.sigmoid(gr) * yr
    out = jnp.dot(merged.astype(BF16), wo_ref[...], preferred_element_type=F32)
    x1 = x_ref[...] + gt_ref[0] * out
    x1_ref[...] = x1
    ms = jnp.mean(x1 * x1, axis=-1, keepdims=True)
    h2 = (x1 * lax.rsqrt(ms + EPS) * g2_ref[...]) * (1.0 + sc_ref[0]) + sh_ref[0]
    h2_ref[...] = h2
    h2_top = _bf16_part(h2)
    h2_hi = h2_top.astype(BF16)
    h2_lo = (h2 - h2_top).astype(BF16)
    wr = wr_ref[...]
    p_hi = jnp.dot(h2_hi, wr, preferred_element_type=F32)
    p_lo = jnp.dot(h2_lo, wr[:, :LANES], preferred_element_type=F32)
    lg_ref[...] = p_hi[:, :LANES] + p_hi[:, LANES:] + p_lo + br_ref[...]


def _merge(att, o_r, proj, x2, gt1, w_up_a, w_up_r, w_out, g2, sc2, sh2, w_router, b_router, seq):
    t, d = x2.shape
    tm = 256
    bpt = seq // tm
    half = d // 2
    cb = half // LANES

    def gspec(cb0, k):
        return pl.BlockSpec((tm, half), lambda i: (i, cb0 // cb + k))

    def const(shape):
        return pl.BlockSpec(shape, lambda i: tuple(0 for _ in shape), pipeline_mode=pl.Buffered(1))

    def per_batch():
        return pl.BlockSpec((1, 1, d), lambda i: (i // bpt, 0, 0))

    return pl.pallas_call(
        _merge_kernel,
        grid=(t // tm,),
        in_specs=[pl.BlockSpec((tm, A_WIDTH), lambda i: (i, 0)),
                  pl.BlockSpec((tm, R_WIDTH), lambda i: (i, 0)),
                  gspec(CB_GA, 0), gspec(CB_GA, 1), gspec(CB_GR, 0), gspec(CB_GR, 1),
                  pl.BlockSpec((tm, d), lambda i: (i, 0)),
                  per_batch(),
                  const((A_WIDTH, d)), const((R_WIDTH, d)), const((d, d)),
                  const((1, d)), per_batch(), per_batch(),
                  const((d, 2 * LANES)), const((1, LANES))],
        out_specs=[pl.BlockSpec((tm, d), lambda i: (i, 0)),
                   pl.BlockSpec((tm, d), lambda i: (i, 0)),
                   pl.BlockSpec((tm, LANES), lambda i: (i, 0))],
        out_shape=[jax.ShapeDtypeStruct((t, d), F32),
                   jax.ShapeDtypeStruct((t, d), F32),
                   jax.ShapeDtypeStruct((t, LANES), F32)],
        compiler_params=_cparams(("arbitrary",)),
        name="merge",
    )(att, o_r, proj, proj, proj, proj, x2, gt1, w_up_a, w_up_r, w_out, g2, sc2, sh2, w_router, b_router)


RT_E0, RT_E1, RT_W0, RT_W1, RT_R0, RT_R1 = 0, 1, 2, 3, 4, 5


def _route_kernel(lg_ref, rt_ref, cnt_ref, carry_scr):
    tr = lg_ref.shape[0]

    @pl.when(pl.program_id(0) == 0)
    def _():
        carry_scr[...] = jnp.zeros_like(carry_scr)

    x = lg_ref[...]
    lane = lax.broadcasted_iota(jnp.int32, (tr, LANES), 1)
    ninf = -jnp.inf

    def lane_max(val):
        return jnp.max(val, axis=-1, keepdims=True)

    def first_lane(mask):
        return jnp.min(jnp.where(mask, lane, LANES), axis=-1, keepdims=True)

    is_g = lane < N_GROUPS
    gmax = lane_max(jnp.where(is_g, x, ninf))
    grp = first_lane(is_g & (x == gmax))
    eg = jnp.where(is_g, jnp.exp(x - gmax), 0.0)
    pg_top = 1.0 / jnp.sum(eg, axis=-1, keepdims=True)

    lo = N_GROUPS + grp * EXPERTS_PER_GROUP
    is_e = (lane >= lo) & (lane < lo + EXPERTS_PER_GROUP)
    emax = lane_max(jnp.where(is_e, x, ninf))
    ee = jnp.where(is_e, jnp.exp(x - emax), 0.0)
    pe = ee / jnp.sum(ee, axis=-1, keepdims=True)
    pe = jnp.where(is_e, pe, ninf)
    p0 = lane_max(pe)
    l0 = first_lane(pe == p0)
    pe1 = jnp.where(lane == l0, ninf, pe)
    p1 = lane_max(pe1)
    l1 = first_lane(pe1 == p1)
    den = p0 + p1
    w0 = pg_top * p0 / den
    w1 = pg_top * p1 / den
    e0 = l0 - N_GROUPS
    e1 = l1 - N_GROUPS

    onehot = ((lane == e0) | (lane == e1)).astype(BF16)
    r = lax.broadcasted_iota(jnp.int32, (tr, tr), 0)
    c = lax.broadcasted_iota(jnp.int32, (tr, tr), 1)
    tri = jnp.where(c < r, 1.0, 0.0).astype(BF16)
    prefix = jnp.dot(tri, onehot, preferred_element_type=F32) + carry_scr[0:1, :]
    rank0 = jnp.sum(jnp.where(lane == e0, prefix, 0.0), axis=-1, keepdims=True)
    rank1 = jnp.sum(jnp.where(lane == e1, prefix, 0.0), axis=-1, keepdims=True)
    total = carry_scr[0:1, :] + jnp.sum(onehot.astype(F32), axis=0, keepdims=True)
    carry_scr[...] = jnp.broadcast_to(total, carry_scr.shape)
    cnt_ref[...] = jnp.broadcast_to(total, cnt_ref.shape)

    rec = jnp.zeros((tr, LANES), F32)
    for k, val in ((RT_E0, e0.astype(F32)), (RT_E1, e1.astype(F32)), (RT_W0, w0), (RT_W1, w1),
                   (RT_R0, rank0), (RT_R1, rank1)):
        rec = jnp.where(lane == k, val, rec)
    rt_ref[...] = rec


def _route(logits):
    t = logits.shape[0]
    tr = 512
    return pl.pallas_call(
        _route_kernel,
        grid=(t // tr,),
        in_specs=[pl.BlockSpec((tr, LANES), lambda i: (i, 0))],
        out_specs=[pl.BlockSpec((tr, LANES), lambda i: (i, 0)),
                   pl.BlockSpec((8, LANES), lambda i: (0, 0))],
        out_shape=[jax.ShapeDtypeStruct((t, LANES), F32),
                   jax.ShapeDtypeStruct((8, LANES), F32)],
        scratch_shapes=[pltpu.VMEM((8, LANES), F32)],
        compiler_params=_cparams(("arbitrary",)),
        name="route",
    )(logits)


def _row_copy(src_hbm, dst_buf, src_row, dst_row, sem):
    return pltpu.make_async_copy(src_hbm.at[pl.ds(src_row, 1), :], dst_buf.at[pl.ds(dst_row, 1), :], sem)


def _wait_rows(src_hbm, dst_buf, n_rows, sem):
    def body(r, _):
        _row_copy(src_hbm, dst_buf, 0, r, sem).wait()
        return 0

    lax.fori_loop(0, n_rows, body, 0, unroll=8)


def _moe_kernel(tok_ref, be_ref, nu_ref, h_hbm, w1_ref, w3_ref, w2_ref, y_ref,
                xbuf_a, xbuf_b, sem, w1_bf, w3_bf, w2_bf):
    rows = MOE_BLOCK
    i = pl.program_id(0)
    nused = nu_ref[0]
    bufs = (xbuf_a, xbuf_b)

    def issue(block, parity):
        base = block * rows
        for r in range(rows):
            _row_copy(h_hbm, bufs[parity], tok_ref[base + r], r, sem.at[parity]).start()

    @pl.when((i == 0) & (nused > 0))
    def _():
        issue(0, 0)

    new_expert = (i == 0) | (be_ref[i] != be_ref[jnp.maximum(i - 1, 0)])

    @pl.when((i < nused) & new_expert)
    def _():
        w1_bf[...] = w1_ref[0].astype(BF16)
        w3_bf[...] = w3_ref[0].astype(BF16)
        w2_bf[...] = w2_ref[0].astype(BF16)

    def block_step(parity, prefetch):
        _wait_rows(h_hbm, bufs[parity], rows, sem.at[parity])
        if prefetch:
            issue(i + 1, 1 - parity)
        xb = bufs[parity][...].astype(BF16)
        h1 = jnp.dot(xb, w1_bf[...], preferred_element_type=F32)
        h3 = jnp.dot(xb, w3_bf[...], preferred_element_type=F32)
        hid = (h1 * jax.nn.sigmoid(h1)) * h3
        y_ref[...] = jnp.dot(hid.astype(BF16), w2_bf[...], preferred_element_type=F32)

    for parity in (0, 1):
        for prefetch in (True, False):
            more = (i + 1 < nused) if prefetch else (i + 1 == nused)
            pl.when((i % 2 == parity) & more)(functools.partial(block_step, parity, prefetch))

    @pl.when(i >= nused)
    def _():
        y_ref[...] = jnp.zeros_like(y_ref)


def _moe(row_tok, blk_e, nused, h2, w1, w3, w2):
    t, d = h2.shape
    r = row_tok.shape[0]
    nblk = r // MOE_BLOCK
    grid_spec = pltpu.PrefetchScalarGridSpec(
        num_scalar_prefetch=3,
        grid=(nblk,),
        in_specs=[pl.BlockSpec(memory_space=pl.ANY),
                  pl.BlockSpec((1, d, D_EXPERT), lambda i, tok, be, nu: (be[i], 0, 0)),
                  pl.BlockSpec((1, d, D_EXPERT), lambda i, tok, be, nu: (be[i], 0, 0)),
                  pl.BlockSpec((1, D_EXPERT, d), lambda i, tok, be, nu: (be[i], 0, 0))],
        out_specs=pl.BlockSpec((MOE_BLOCK, d), lambda i, tok, be, nu: (i, 0)),
        scratch_shapes=[pltpu.VMEM((MOE_BLOCK, d), F32), pltpu.VMEM((MOE_BLOCK, d), F32),
                        pltpu.SemaphoreType.DMA((2,)),
                        pltpu.VMEM((d, D_EXPERT), BF16), pltpu.VMEM((d, D_EXPERT), BF16),
                        pltpu.VMEM((D_EXPERT, d), BF16)],
    )
    return pl.pallas_call(
        _moe_kernel,
        grid_spec=grid_spec,
        out_shape=jax.ShapeDtypeStruct((r, d), F32),
        compiler_params=_cparams(("arbitrary",)),
        name="moe",
    )(row_tok, blk_e, nused, h2, w1, w3, w2)


def _final_kernel(dest_ref, y_hbm, x1_ref, rt_ref, gt_ref, g_ref, o_ref, ybuf_a, ybuf_b, sem, *, last_layer):
    tf = x1_ref.shape[0]
    i = pl.program_id(0)
    n = pl.num_programs(0)
    bufs = (ybuf_a, ybuf_b)

    def issue(step, parity):
        base = step * tf * TOP_K_INNER
        for r in range(tf):
            for k in range(TOP_K_INNER):
                _row_copy(y_hbm, bufs[parity].at[k], dest_ref[base + r * TOP_K_INNER + k], r,
                          sem.at[parity]).start()

    @pl.when(i == 0)
    def _():
        issue(0, 0)

    def tile_step(parity, prefetch):
        buf = bufs[parity]
        for k in range(TOP_K_INNER):
            _wait_rows(y_hbm, buf.at[k], tf, sem.at[parity])
        if prefetch:
            issue(i + 1, 1 - parity)
        rt = rt_ref[...]
        w0 = rt[:, RT_W0:RT_W0 + 1]
        w1 = rt[:, RT_W1:RT_W1 + 1]
        y = buf[0] * w0 + buf[1] * w1
        x2 = x1_ref[...] + gt_ref[0] * y
        if last_layer:
            ms = jnp.mean(x2 * x2, axis=-1, keepdims=True)
            x2 = x2 * lax.rsqrt(ms + EPS) * g_ref[...]
        o_ref[...] = x2

    for parity in (0, 1):
        for prefetch in (True, False):
            more = (i + 1 < n) if prefetch else (i + 1 == n)
            pl.when((i % 2 == parity) & more)(functools.partial(tile_step, parity, prefetch))


def _final(dest, yr, x1, rt, gt2, final_g, seq, last_layer):
    t, d = x1.shape
    tf = 256
    bpt = seq // tf
    grid_spec = pltpu.PrefetchScalarGridSpec(
        num_scalar_prefetch=1,
        grid=(t // tf,),
        in_specs=[pl.BlockSpec(memory_space=pl.ANY),
                  pl.BlockSpec((tf, d), lambda i, dst: (i, 0)),
                  pl.BlockSpec((tf, LANES), lambda i, dst: (i, 0)),
                  pl.BlockSpec((1, 1, d), lambda i, dst: (i // bpt, 0, 0)),
                  pl.BlockSpec((1, d), lambda i, dst: (0, 0))],
        out_specs=pl.BlockSpec((tf, d), lambda i, dst: (i, 0)),
        scratch_shapes=[pltpu.VMEM((TOP_K_INNER, tf, d), F32), pltpu.VMEM((TOP_K_INNER, tf, d), F32),
                        pltpu.SemaphoreType.DMA((2,))],
    )
    return pl.pallas_call(
        functools.partial(_final_kernel, last_layer=last_layer),
        grid_spec=grid_spec,
        out_shape=jax.ShapeDtypeStruct((t, d), F32),
        compiler_params=_cparams(("arbitrary",)),
        name="final",
    )(dest, yr, x1, rt, gt2, final_g)


def _dispatch_tables(rt, counts_row):
    t = rt.shape[0]
    tk = t * TOP_K_INNER
    eid = rt[:, RT_E0:RT_E1 + 1].astype(jnp.int32)
    rank = rt[:, RT_R0:RT_R1 + 1].astype(jnp.int32)
    counts = counts_row[:N_EXPERTS].astype(jnp.int32)
    padded = (counts + MOE_BLOCK - 1) // MOE_BLOCK * MOE_BLOCK
    pad_end = jnp.cumsum(padded)
    pad_start = pad_end - padded
    dest = (pad_start[eid] + rank).reshape(tk)
    r = tk + N_EXPERTS * MOE_BLOCK
    tok = jnp.repeat(jnp.arange(t, dtype=jnp.int32), TOP_K_INNER)
    row_tok = jnp.zeros((r,), jnp.int32).at[dest].set(tok)
    nblk = r // MOE_BLOCK
    blk_start = jnp.arange(nblk, dtype=jnp.int32) * MOE_BLOCK
    blk_e = jnp.minimum(jnp.sum(pad_end[None, :] <= blk_start[:, None], axis=1), N_EXPERTS - 1).astype(jnp.int32)
    nused = (pad_end[-1:] // MOE_BLOCK).astype(jnp.int32)
    return dest.astype(jnp.int32), row_tok, blk_e, nused


def kernel(x, c, norm1_g, norm2_g, final_g, w_ada, b_ada, w_in, r_lower, r_norm_g,
           w_up_a, w_up_r, w_out, w_rg, b_rg, w_re, b_re, w1, w3, w2):
    batch, seq, d = x.shape
    t = batch * seq
    depth = w_in.shape[0]
    x2 = x.reshape(t, d)
    c_pad = jnp.zeros((8, d), F32).at[:batch].set(c)
    for l in range(depth):
        mod = _ada(c_pad, w_ada[l], b_ada[l][None, :])[:batch]
        sh1, sc1, gt1, sh2, sc2, gt2 = [m[:, None, :] for m in jnp.split(mod, 6, axis=-1)]
        proj = _proj(x2, norm1_g[l][None, :], sc1, sh1, w_in[l], seq)
        att = _moba(proj, batch, seq)
        o_r = _hgrn(proj, r_lower, r_norm_g[l][None, :], batch, seq, l)
        w_router = jnp.zeros((d, LANES), F32).at[:, :N_GROUPS].set(w_rg[l]) \
            .at[:, N_GROUPS:N_GROUPS + N_EXPERTS].set(w_re[l])
        b_router = jnp.zeros((1, LANES), F32).at[0, :N_GROUPS].set(b_rg[l]) \
            .at[0, N_GROUPS:N_GROUPS + N_EXPERTS].set(b_re[l])
        w_router_top = _bf16_part(w_router)
        w_router = jnp.concatenate([w_router_top, w_router - w_router_top], axis=-1).astype(BF16)
        x1, h2, logits = _merge(att, o_r, proj, x2, gt1, w_up_a[l].astype(BF16), w_up_r[l].astype(BF16),
                                w_out[l].astype(BF16), norm2_g[l][None, :], sc2, sh2, w_router, b_router, seq)
        rt, cnt = _route(logits)
        dest, row_tok, blk_e, nused = _dispatch_tables(rt, cnt[0])
        yr = _moe(row_tok, blk_e, nused, h2, w1[l], w3[l], w2[l])
        x2 = _final(dest, yr, x1, rt, gt2, final_g[None, :], seq, l == depth - 1)
    return x2.reshape(batch, seq, d)
```

```python
import functools

import jax
import jax.numpy as jnp
from jax import lax
from jax.experimental import pallas as pl
from jax.experimental.pallas import tpu as pltpu

F32 = jnp.float32
BF16 = jnp.bfloat16
HIGHEST = lax.Precision.HIGHEST

D_MODEL = 2048
A_HEADS = 8
A_HEAD_DIM = 128
A_WIDTH = A_HEADS * A_HEAD_DIM
MOBA_BLOCK = 256
MOBA_TOPK = 3
R_HEADS = 8
R_KEY_DIM = 128
R_VAL_DIM = 128
R_WIDTH = R_HEADS * R_KEY_DIM
N_GROUPS = 4
EXPERTS_PER_GROUP = 8
N_EXPERTS = N_GROUPS * EXPERTS_PER_GROUP
TOP_K_INNER = 2
D_EXPERT = 512
MOE_BLOCK = 256
EPS = 1e-6
IN_COLS = 3 * A_WIDTH + 4 * R_WIDTH + 2 * D_MODEL

LANES = 128
SUBLANES = 8
CB_AQ = 0
CB_AK = CB_AQ + A_WIDTH // LANES
CB_AV = CB_AK + A_WIDTH // LANES
CB_RQ = CB_AV + A_WIDTH // LANES
CB_RF = CB_RQ + R_WIDTH // LANES
CB_RI = CB_RF + R_WIDTH // LANES
CB_ROG = CB_RI + R_WIDTH // LANES
CB_GA = CB_ROG + R_WIDTH // LANES
CB_GR = CB_GA + D_MODEL // LANES

MOBA_GROUP = 4
Q_PRESCALE = A_HEAD_DIM ** -0.5 * 1.4426950408889634
HGRN_TILE = 512
HGRN_HEADS_PER_STEP = 2
HGRN_CHUNK = 64
HGRN_SUB = 8
GATHER_LOOKAHEAD = 2
NEG_BIG = -1e30
VMEM_LIMIT = 56 * 1024 * 1024


def _cparams(sem):
    return pltpu.CompilerParams(dimension_semantics=sem, vmem_limit_bytes=VMEM_LIMIT)


def _dot_nt(a, b, **kw):
    return lax.dot_general(a, b, (((1,), (1,)), ((), ())), preferred_element_type=F32, **kw)


def _bf16_part(a):
    bits = lax.bitcast_convert_type(a, jnp.uint32) & jnp.uint32(0xFFFF0000)
    return lax.bitcast_convert_type(bits, F32)


def _dot_tn(a, b):
    return lax.dot_general(a, b, (((0,), (0,)), ((), ())), preferred_element_type=F32)


def _ada_kernel(c_ref, w_ref, b_ref, o_ref):
    c = c_ref[...]
    ca = c * jax.nn.sigmoid(c)
    o_ref[...] = jnp.dot(ca, w_ref[...], preferred_element_type=F32, precision=HIGHEST) + b_ref[...]


def _ada(c_pad, w_ada, b_ada):
    rows, d = c_pad.shape
    n = w_ada.shape[1]
    tn = 1024
    return pl.pallas_call(
        _ada_kernel,
        grid=(n // tn,),
        in_specs=[pl.BlockSpec((rows, d), lambda j: (0, 0)),
                  pl.BlockSpec((d, tn), lambda j: (0, j)),
                  pl.BlockSpec((1, tn), lambda j: (0, j))],
        out_specs=pl.BlockSpec((rows, tn), lambda j: (0, j)),
        out_shape=jax.ShapeDtypeStruct((rows, n), F32),
        compiler_params=_cparams(("arbitrary",)),
        name="ada",
    )(c_pad, w_ada, b_ada)


def _proj_kernel(x_ref, g_ref, sc_ref, sh_ref, w_ref, cs_ref, o_ref, h_scr):
    @pl.when(pl.program_id(1) == 0)
    def _():
        x = x_ref[...]
        ms = jnp.mean(x * x, axis=-1, keepdims=True)
        y = x * lax.rsqrt(ms + EPS) * g_ref[...]
        h_scr[...] = (y * (1.0 + sc_ref[0]) + sh_ref[0]).astype(BF16)

    acc = jnp.dot(h_scr[...], w_ref[...].astype(BF16), preferred_element_type=F32)
    o_ref[...] = (acc * cs_ref[...]).astype(o_ref.dtype)


def _proj(x2, g, sc, sh, w_in, seq):
    t, d = x2.shape
    n = w_in.shape[1]
    tm = min(1024, seq)
    tn = 1024
    bpt = seq // tm
    col_scale = jnp.ones((1, n), F32).at[:, CB_AQ * LANES:CB_AK * LANES].set(Q_PRESCALE)
    return pl.pallas_call(
        _proj_kernel,
        grid=(t // tm, n // tn),
        in_specs=[pl.BlockSpec((tm, d), lambda i, j: (i, 0)),
                  pl.BlockSpec((1, d), lambda i, j: (0, 0)),
                  pl.BlockSpec((1, 1, d), lambda i, j: (i // bpt, 0, 0)),
                  pl.BlockSpec((1, 1, d), lambda i, j: (i // bpt, 0, 0)),
                  pl.BlockSpec((d, tn), lambda i, j: (0, j)),
                  pl.BlockSpec((1, tn), lambda i, j: (0, j))],
        out_specs=pl.BlockSpec((tm, tn), lambda i, j: (i, j)),
        out_shape=jax.ShapeDtypeStruct((t, n), BF16),
        scratch_shapes=[pltpu.VMEM((tm, d), BF16)],
        compiler_params=_cparams(("arbitrary", "arbitrary")),
        name="proj",
    )(x2, g, sc, sh, w_in, col_scale)


def _moba_kernel(q_ref, k_ref, v_ref, o_ref, kaug_scr, vt_scr, kmean_scr, s_scr, *, nb, gr):
    blk = MOBA_BLOCK
    gk = MOBA_GROUP * blk
    seq = k_ref.shape[0]

    k = k_ref[...]
    kaug_scr[:, :LANES] = k
    row_blk = lax.broadcasted_iota(jnp.int32, (seq, LANES), 0) // blk
    lane = lax.broadcasted_iota(jnp.int32, (seq, LANES), 1)
    kaug_scr[:, LANES:] = jnp.where(lane == row_blk, 1.0, 0.0).astype(BF16)
    kmean_scr[...] = jnp.zeros_like(kmean_scr)
    kmean_scr[0:nb, :] = jnp.mean(k.astype(F32).reshape(nb, blk, LANES), axis=1)
    for j in range(nb):
        vj = v_ref[j * blk:(j + 1) * blk, :].astype(F32)
        vt_scr[0:LANES, j * blk:(j + 1) * blk] = vj.T.astype(BF16)
    vt_scr[LANES:LANES + 16, :] = jnp.where(
        lax.broadcasted_iota(jnp.int32, (16, seq), 0) == 0, 1.0, 0.0).astype(BF16)

    rid = lax.broadcasted_iota(jnp.int32, (gr, blk), 0)

    def scores(i):
        q = q_ref[i * blk:(i + 1) * blk, :]
        gate = _dot_nt(kmean_scr[...], q.astype(F32), precision=HIGHEST)
        gate = jnp.where(rid < i, gate, -jnp.inf)
        sel = rid == i
        for _ in range(MOBA_TOPK):
            m = jnp.max(gate, axis=0, keepdims=True)
            idx = jnp.min(jnp.where(gate == m, rid, gr), axis=0, keepdims=True)
            pick = rid == idx
            sel = sel | (pick & (rid < i))
            gate = jnp.where(pick, -jnp.inf, gate)
        bias_t = jnp.where(sel, 0.0, NEG_BIG)
        bias_t = jnp.concatenate([bias_t, jnp.zeros((LANES - gr, blk), F32)], axis=0)
        q_aug = jnp.concatenate([q, bias_t.T.astype(BF16)], axis=-1)
        n = (i + 1) * blk
        spans = [(r0, min(r0 + gk, n)) for r0 in range(0, n, gk)]
        slot = i % 2
        mx = jnp.full((8, blk), NEG_BIG, F32)
        for r0, r1 in spans:
            s = _dot_nt(kaug_scr[r0:r1, :], q_aug)
            if r1 == n:
                kpos = r0 + lax.broadcasted_iota(jnp.int32, (r1 - r0, blk), 0)
                qpos = i * blk + lax.broadcasted_iota(jnp.int32, (r1 - r0, blk), 1)
                s = jnp.where(kpos <= qpos, s, NEG_BIG)
            s_scr[slot, r0:r1, :] = s
            mx = jnp.maximum(mx, jnp.max(s.reshape((r1 - r0) // 8, 8, blk), axis=0))
        return jnp.max(mx, axis=0, keepdims=True), spans

    def weighted_values(i, m, spans):
        slot = i % 2
        acc = jnp.zeros((LANES + 16, blk), F32)
        for r0, r1 in spans:
            p = jnp.exp2(s_scr[slot, r0:r1, :] - m).astype(BF16)
            acc = acc + jnp.dot(vt_scr[:, r0:r1], p, preferred_element_type=F32)
        out_t = acc[0:LANES] / acc[LANES:LANES + 1]
        o_ref[i * blk:(i + 1) * blk, :] = out_t.T.astype(o_ref.dtype)

    pending = scores(0)
    for i in range(nb):
        nxt = scores(i + 1) if i + 1 < nb else None
        weighted_values(i, *pending)
        pending = nxt


def _moba(proj, batch, seq):
    nb = seq // MOBA_BLOCK
    blk = MOBA_BLOCK
    gr = -(-nb // 8) * 8
    t = batch * seq
    assert gr <= LANES
    return pl.pallas_call(
        functools.partial(_moba_kernel, nb=nb, gr=gr),
        grid=(batch, A_HEADS),
        in_specs=[pl.BlockSpec((seq, LANES), lambda b, h: (b, CB_AQ + h)),
                  pl.BlockSpec((seq, LANES), lambda b, h: (b, CB_AK + h)),
                  pl.BlockSpec((seq, LANES), lambda b, h: (b, CB_AV + h))],
        out_specs=pl.BlockSpec((seq, LANES), lambda b, h: (b, h)),
        out_shape=jax.ShapeDtypeStruct((t, A_WIDTH), BF16),
        scratch_shapes=[pltpu.VMEM((seq, 2 * LANES), BF16),
                        pltpu.VMEM((LANES + 16, seq), BF16),
                        pltpu.VMEM((gr, LANES), F32),
                        pltpu.VMEM((2, seq, blk), F32)],
        compiler_params=_cparams(("arbitrary", "arbitrary")),
        name="moba",
    )(proj, proj, proj)


def _hgrn_kernel(q_ref, f_ref, i_ref, og_ref, rl_ref, g_ref, o_ref, st_scr, b_scr, *, layer):
    @pl.when(pl.program_id(2) == 0)
    def _():
        st_scr[...] = jnp.zeros_like(st_scr)

    for hh in range(q_ref.shape[1] // LANES):
        cs = slice(hh * LANES, (hh + 1) * LANES)
        _hgrn_head(q_ref.at[:, cs], f_ref.at[:, cs], i_ref.at[:, cs], og_ref.at[:, cs], rl_ref.at[:, cs],
                   g_ref, o_ref.at[:, cs], st_scr.at[hh], b_scr.at[hh], layer=layer)


def _hgrn_head(q_ref, f_ref, i_ref, og_ref, rl_ref, g_ref, o_ref, st_scr, b_scr, *, layer):
    tt = q_ref.shape[0]
    ch, sub = HGRN_CHUNK, HGRN_SUB
    ns, nc = ch // sub, tt // ch
    assert SUBLANES % sub == 0 and ch % SUBLANES == 0 and tt % ch == 0

    rl = rl_ref[...]
    e = jnp.exp(rl - jnp.max(rl, axis=0, keepdims=True))
    lb = jnp.sum(e[: layer + 1], axis=0, keepdims=True) / jnp.sum(e, axis=0, keepdims=True)

    q = q_ref[...].astype(F32)
    x = f_ref[...].astype(F32)
    v_bf = i_ref[...]
    v = v_bf.astype(F32)
    f = lb + (1.0 - lb) * jax.nn.sigmoid(x)
    kin = (1.0 - lb) * jax.nn.sigmoid(-x)
    row = lax.broadcasted_iota(jnp.int32, (tt, LANES), 0)
    pic = row % ch
    pos = row % sub

    def roll8(a, k):
        return pltpu.roll(a.reshape(tt // SUBLANES, SUBLANES, LANES), k, 1).reshape(tt, LANES)

    b = jnp.log2(f)
    pos8 = row % SUBLANES
    step = 1
    while step < SUBLANES:
        b = b + jnp.where(pos8 >= step, roll8(b, step), 0.0)
        step *= 2
    gpc = ch // SUBLANES
    b4 = b.reshape(nc, gpc, SUBLANES, LANES)
    offs = [jnp.zeros((nc, 1, 1, LANES), F32)]
    for g in range(1, gpc):
        offs.append(offs[-1] + b4[:, g - 1:g, SUBLANES - 1:SUBLANES, :])
    b = (b4 + jnp.concatenate(offs, axis=1)).reshape(tt, LANES)
    b_scr[...] = b

    r_sub = b
    for k in range(1, sub):
        r_sub = jnp.where(pos == sub - 1 - k, roll8(b, SUBLANES - k), r_sub)
    kh = kin * jnp.exp2(r_sub - b)

    diag = jnp.sum(q * kin, axis=-1, keepdims=True) * v
    for delta in range(1, sub):
        kr = roll8(kin, delta)
        br = roll8(b, delta)
        vr = roll8(v, delta)
        w = jnp.where(pos >= delta, q * kr * jnp.exp2(b - br), 0.0)
        diag = diag + jnp.sum(w, axis=-1, keepdims=True) * vr

    pc = lax.broadcasted_iota(jnp.int32, (ch, LANES), 0)
    sid = pc // sub
    st = st_scr[...]
    outs = []
    for c in range(nc):
        sl = slice(c * ch, (c + 1) * ch)
        q_c, b_c, kh_c, vc = q[sl], b[sl], kh[sl], v_bf[sl]
        q_blocks, k_blocks = [], []
        for j in range(ns - 1):
            rj = b_scr[pl.ds(c * ch + sub * (j + 1) - 1, 1), :]
            qj = q_c * jnp.exp2(jnp.where(pc >= sub * (j + 1), b_c - rj, -jnp.inf))
            q_blocks.append(qj.astype(BF16))
            k_blocks.append(jnp.where(sid == j, kh_c, 0.0).astype(BF16))
        a_off = _dot_nt(jnp.concatenate(q_blocks, axis=-1), jnp.concatenate(k_blocks, axis=-1))
        o_c = jnp.dot(a_off.astype(BF16), vc, preferred_element_type=F32)
        o_c = o_c + _dot_nt((q_c * jnp.exp2(b_c)).astype(BF16), st.astype(BF16)) + diag[sl]
        r_last = b_scr[pl.ds((c + 1) * ch - 1, 1), :]
        ke = (kh_c * jnp.exp2(r_last - r_sub[sl])).astype(BF16)
        st = jnp.exp2(r_last) * st + _dot_tn(vc, ke)
        outs.append(o_c)
    st_scr[...] = st
    o = jnp.concatenate(outs, axis=0)

    y = o * lax.rsqrt(jnp.mean(o * o, axis=-1, keepdims=True) + EPS) * g_ref[...]
    og = og_ref[...].astype(F32)
    o_ref[...] = (y * (og * jax.nn.sigmoid(og))).astype(o_ref.dtype)


def _hgrn(proj, r_lower, r_norm_g, batch, seq, layer):
    tt = min(HGRN_TILE, seq)
    nt = seq // tt
    t = batch * seq
    nl = r_lower.shape[0]

    nh = HGRN_HEADS_PER_STEP
    hw = nh * LANES
    assert R_HEADS % nh == 0 and all(cb % nh == 0 for cb in (CB_RQ, CB_RF, CB_RI, CB_ROG))

    def col(cb):
        return pl.BlockSpec((tt, hw), lambda b, h, c: (b * nt + c, cb // nh + h))

    return pl.pallas_call(
        functools.partial(_hgrn_kernel, layer=layer),
        grid=(batch, R_HEADS // nh, nt),
        in_specs=[col(CB_RQ), col(CB_RF), col(CB_RI), col(CB_ROG),
                  pl.BlockSpec((nl, hw), lambda b, h, c: (0, h)),
                  pl.BlockSpec((1, LANES), lambda b, h, c: (0, 0))],
        out_specs=pl.BlockSpec((tt, hw), lambda b, h, c: (b * nt + c, h)),
        out_shape=jax.ShapeDtypeStruct((t, R_WIDTH), BF16),
        scratch_shapes=[pltpu.VMEM((nh, LANES, LANES), F32), pltpu.VMEM((nh, tt, LANES), F32)],
        compiler_params=_cparams(("arbitrary", "arbitrary", "arbitrary")),
        name="hgrn",
    )(proj, proj, proj, proj, r_lower, r_norm_g)


def _merge_kernel(att_ref, or_ref, ga0_ref, ga1_ref, gr0_ref, gr1_ref, x_ref, gt_ref,
                  wua_ref, wur_ref, wo_ref, g2_ref, sc_ref, sh_ref, wr_ref, br_ref,
                  x1_ref, h2_ref, lg_ref):
    ya = jnp.dot(att_ref[...], wua_ref[...], preferred_element_type=F32)
    yr = jnp.dot(or_ref[...], wur_ref[...], preferred_element_type=F32)
    ga = jnp.concatenate([ga0_ref[...], ga1_ref[...]], axis=-1).astype(F32)
    gr = jnp.concatenate([gr0_ref[...], gr1_ref[...]], axis=-1).astype(F32)
    merged = jax.nn.sigmoid(ga) * ya + jax.nn.sigmoid(gr) * yr
    out = jnp.dot(merged.astype(BF16), wo_ref[...], preferred_element_type=F32)
    x1 = x_ref[...] + gt_ref[0] * out
    x1_ref[...] = x1
    ms = jnp.mean(x1 * x1, axis=-1, keepdims=True)
    h2 = (x1 * lax.rsqrt(ms + EPS) * g2_ref[...]) * (1.0 + sc_ref[0]) + sh_ref[0]
    h2_ref[...] = h2
    h2_top = _bf16_part(h2)
    h2_hi = h2_top.astype(BF16)
    h2_lo = (h2 - h2_top).astype(BF16)
    wr = wr_ref[...]
    p_hi = jnp.dot(h2_hi, wr, preferred_element_type=F32)
    p_lo = jnp.dot(h2_lo, wr[:, :LANES], preferred_element_type=F32)
    lg_ref[...] = p_hi[:, :LANES] + p_hi[:, LANES:] + p_lo + br_ref[...]


def _merge(att, o_r, proj, x2, gt1, w_up_a, w_up_r, w_out, g2, sc2, sh2, w_router, b_router, seq):
    t, d = x2.shape
    tm = 256
    bpt = seq // tm
    half = d // 2
    cb = half // LANES

    def gspec(cb0, k):
        return pl.BlockSpec((tm, half), lambda i: (i, cb0 // cb + k))

    def const(shape):
        return pl.BlockSpec(shape, lambda i: tuple(0 for _ in shape), pipeline_mode=pl.Buffered(1))

    def per_batch():
        return pl.BlockSpec((1, 1, d), lambda i: (i // bpt, 0, 0))

    return pl.pallas_call(
        _merge_kernel,
        grid=(t // tm,),
        in_specs=[pl.BlockSpec((tm, A_WIDTH), lambda i: (i, 0)),
                  pl.BlockSpec((tm, R_WIDTH), lambda i: (i, 0)),
                  gspec(CB_GA, 0), gspec(CB_GA, 1), gspec(CB_GR, 0), gspec(CB_GR, 1),
                  pl.BlockSpec((tm, d), lambda i: (i, 0)),
                  per_batch(),
                  const((A_WIDTH, d)), const((R_WIDTH, d)), const((d, d)),
                  const((1, d)), per_batch(), per_batch(),
                  const((d, 2 * LANES)), const((1, LANES))],
        out_specs=[pl.BlockSpec((tm, d), lambda i: (i, 0)),
                   pl.BlockSpec((tm, d), lambda i: (i, 0)),
                   pl.BlockSpec((tm, LANES), lambda i: (i, 0))],
        out_shape=[jax.ShapeDtypeStruct((t, d), F32),
                   jax.ShapeDtypeStruct((t, d), F32),
                   jax.ShapeDtypeStruct((t, LANES), F32)],
        compiler_params=_cparams(("arbitrary",)),
        name="merge",
    )(att, o_r, proj, proj, proj, proj, x2, gt1, w_up_a, w_up_r, w_out, g2, sc2, sh2, w_router, b_router)


RT_E0, RT_E1, RT_W0, RT_W1, RT_R0, RT_R1 = 0, 1, 2, 3, 4, 5


def _route_kernel(lg_ref, rt_ref, cnt_ref, carry_scr):
    tr = lg_ref.shape[0]

    @pl.when(pl.program_id(0) == 0)
    def _():
        carry_scr[...] = jnp.zeros_like(carry_scr)

    x = lg_ref[...]
    lane = lax.broadcasted_iota(jnp.int32, (tr, LANES), 1)
    ninf = -jnp.inf

    def lane_max(val):
        return jnp.max(val, axis=-1, keepdims=True)

    def first_lane(mask):
        return jnp.min(jnp.where(mask, lane, LANES), axis=-1, keepdims=True)

    is_g = lane < N_GROUPS
    gmax = lane_max(jnp.where(is_g, x, ninf))
    grp = first_lane(is_g & (x == gmax))
    eg = jnp.where(is_g, jnp.exp(x - gmax), 0.0)
    pg_top = 1.0 / jnp.sum(eg, axis=-1, keepdims=True)

    lo = N_GROUPS + grp * EXPERTS_PER_GROUP
    is_e = (lane >= lo) & (lane < lo + EXPERTS_PER_GROUP)
    emax = lane_max(jnp.where(is_e, x, ninf))
    ee = jnp.where(is_e, jnp.exp(x - emax), 0.0)
    pe = ee / jnp.sum(ee, axis=-1, keepdims=True)
    pe = jnp.where(is_e, pe, ninf)
    p0 = lane_max(pe)
    l0 = first_lane(pe == p0)
    pe1 = jnp.where(lane == l0, ninf, pe)
    p1 = lane_max(pe1)
    l1 = first_lane(pe1 == p1)
    den = p0 + p1
    w0 = pg_top * p0 / den
    w1 = pg_top * p1 / den
    e0 = l0 - N_GROUPS
    e1 = l1 - N_GROUPS

    onehot = ((lane == e0) | (lane == e1)).astype(BF16)
    r = lax.broadcasted_iota(jnp.int32, (tr, tr), 0)
    c = lax.broadcasted_iota(jnp.int32, (tr, tr), 1)
    tri = jnp.where(c < r, 1.0, 0.0).astype(BF16)
    prefix = jnp.dot(tri, onehot, preferred_element_type=F32) + carry_scr[0:1, :]
    rank0 = jnp.sum(jnp.where(lane == e0, prefix, 0.0), axis=-1, keepdims=True)
    rank1 = jnp.sum(jnp.where(lane == e1, prefix, 0.0), axis=-1, keepdims=True)
    total = carry_scr[0:1, :] + jnp.sum(onehot.astype(F32), axis=0, keepdims=True)
    carry_scr[...] = jnp.broadcast_to(total, carry_scr.shape)
    cnt_ref[...] = jnp.broadcast_to(total, cnt_ref.shape)

    rec = jnp.zeros((tr, LANES), F32)
    for k, val in ((RT_E0, e0.astype(F32)), (RT_E1, e1.astype(F32)), (RT_W0, w0), (RT_W1, w1),
                   (RT_R0, rank0), (RT_R1, rank1)):
        rec = jnp.where(lane == k, val, rec)
    rt_ref[...] = rec


def _route(logits):
    t = logits.shape[0]
    tr = 512
    return pl.pallas_call(
        _route_kernel,
        grid=(t // tr,),
        in_specs=[pl.BlockSpec((tr, LANES), lambda i: (i, 0))],
        out_specs=[pl.BlockSpec((tr, LANES), lambda i: (i, 0)),
                   pl.BlockSpec((8, LANES), lambda i: (0, 0))],
        out_shape=[jax.ShapeDtypeStruct((t, LANES), F32),
                   jax.ShapeDtypeStruct((8, LANES), F32)],
        scratch_shapes=[pltpu.VMEM((8, LANES), F32)],
        compiler_params=_cparams(("arbitrary",)),
        name="route",
    )(logits)


def _row_copy(src_hbm, dst_buf, src_row, dst_row, sem):
    return pltpu.make_async_copy(src_hbm.at[pl.ds(src_row, 1), :], dst_buf.at[pl.ds(dst_row, 1), :], sem)


def _wait_rows(src_hbm, dst_buf, n_rows, sem):
    def body(r, _):
        _row_copy(src_hbm, dst_buf, 0, r, sem).wait()
        return 0

    lax.fori_loop(0, n_rows, body, 0, unroll=8)


def _moe_kernel(tok_ref, be_ref, nu_ref, h_hbm, w1_ref, w3_ref, w2_ref, y_ref,
                xbuf_a, xbuf_b, xbuf_c, sem, w1_bf, w3_bf, w2_bf):
    rows = MOE_BLOCK
    i = pl.program_id(0)
    nused = nu_ref[0]
    bufs = (xbuf_a, xbuf_b, xbuf_c)
    depth = GATHER_LOOKAHEAD

    def issue(block, parity):
        base = block * rows
        for r in range(rows):
            _row_copy(h_hbm, bufs[parity], tok_ref[base + r], r, sem.at[parity]).start()

    for first in range(depth):
        @pl.when((i == 0) & (first < nused))
        def _(first=first):
            issue(first, first)

    new_expert = (i == 0) | (be_ref[i] != be_ref[jnp.maximum(i - 1, 0)])

    @pl.when((i < nused) & new_expert)
    def _():
        w1_bf[...] = w1_ref[0].astype(BF16)
        w3_bf[...] = w3_ref[0].astype(BF16)
        w2_bf[...] = w2_ref[0].astype(BF16)

    def block_step(parity, prefetch):
        _wait_rows(h_hbm, bufs[parity], rows, sem.at[parity])
        if prefetch:
            issue(i + depth, (parity + depth) % len(bufs))
        xb = bufs[parity][...].astype(BF16)
        h1 = jnp.dot(xb, w1_bf[...], preferred_element_type=F32)
        h3 = jnp.dot(xb, w3_bf[...], preferred_element_type=F32)
        hid = (h1 * jax.nn.sigmoid(h1)) * h3
        y_ref[...] = jnp.dot(hid.astype(BF16), w2_bf[...], preferred_element_type=F32)

    for parity in range(len(bufs)):
        for prefetch in (True, False):
            more = (i + depth < nused) if prefetch else ((i < nused) & (i + depth >= nused))
            pl.when((i % len(bufs) == parity) & more)(functools.partial(block_step, parity, prefetch))

    @pl.when(i >= nused)
    def _():
        y_ref[...] = jnp.zeros_like(y_ref)


def _moe(row_tok, blk_e, nused, h2, w1, w3, w2):
    t, d = h2.shape
    r = row_tok.shape[0]
    nblk = r // MOE_BLOCK
    grid_spec = pltpu.PrefetchScalarGridSpec(
        num_scalar_prefetch=3,
        grid=(nblk,),
        in_specs=[pl.BlockSpec(memory_space=pl.ANY),
                  pl.BlockSpec((1, d, D_EXPERT), lambda i, tok, be, nu: (be[i], 0, 0)),
                  pl.BlockSpec((1, d, D_EXPERT), lambda i, tok, be, nu: (be[i], 0, 0)),
                  pl.BlockSpec((1, D_EXPERT, d), lambda i, tok, be, nu: (be[i], 0, 0))],
        out_specs=pl.BlockSpec((MOE_BLOCK, d), lambda i, tok, be, nu: (i, 0)),
        scratch_shapes=[pltpu.VMEM((MOE_BLOCK, d), F32)] * (GATHER_LOOKAHEAD + 1) + [
                        pltpu.SemaphoreType.DMA((GATHER_LOOKAHEAD + 1,)),
                        pltpu.VMEM((d, D_EXPERT), BF16), pltpu.VMEM((d, D_EXPERT), BF16),
                        pltpu.VMEM((D_EXPERT, d), BF16)],
    )
    return pl.pallas_call(
        _moe_kernel,
        grid_spec=grid_spec,
        out_shape=jax.ShapeDtypeStruct((r, d), F32),
        compiler_params=_cparams(("arbitrary",)),
        name="moe",
    )(row_tok, blk_e, nused, h2, w1, w3, w2)


def _final_kernel(dest_ref, y_hbm, x1_ref, rt_ref, gt_ref, g_ref, o_ref, ybuf_a, ybuf_b, ybuf_c, sem, *,
                  last_layer):
    tf = x1_ref.shape[0]
    i = pl.program_id(0)
    n = pl.num_programs(0)
    bufs = (ybuf_a, ybuf_b, ybuf_c)
    depth = GATHER_LOOKAHEAD

    def issue(step, parity):
        base = step * tf * TOP_K_INNER
        for r in range(tf):
            for k in range(TOP_K_INNER):
                _row_copy(y_hbm, bufs[parity].at[k], dest_ref[base + r * TOP_K_INNER + k], r,
                          sem.at[parity]).start()

    for first in range(depth):
        @pl.when((i == 0) & (first < n))
        def _(first=first):
            issue(first, first)

    def tile_step(parity, prefetch):
        buf = bufs[parity]
        for k in range(TOP_K_INNER):
            _wait_rows(y_hbm, buf.at[k], tf, sem.at[parity])
        if prefetch:
            issue(i + depth, (parity + depth) % len(bufs))
        rt = rt_ref[...]
        w0 = rt[:, RT_W0:RT_W0 + 1]
        w1 = rt[:, RT_W1:RT_W1 + 1]
        y = buf[0] * w0 + buf[1] * w1
        x2 = x1_ref[...] + gt_ref[0] * y
        if last_layer:
            ms = jnp.mean(x2 * x2, axis=-1, keepdims=True)
            x2 = x2 * lax.rsqrt(ms + EPS) * g_ref[...]
        o_ref[...] = x2

    for parity in range(len(bufs)):
        for prefetch in (True, False):
            more = (i + depth < n) if prefetch else (i + depth >= n)
            pl.when((i % len(bufs) == parity) & more)(functools.partial(tile_step, parity, prefetch))


def _final(dest, yr, x1, rt, gt2, final_g, seq, last_layer):
    t, d = x1.shape
    tf = 256
    bpt = seq // tf
    grid_spec = pltpu.PrefetchScalarGridSpec(
        num_scalar_prefetch=1,
        grid=(t // tf,),
        in_specs=[pl.BlockSpec(memory_space=pl.ANY),
                  pl.BlockSpec((tf, d), lambda i, dst: (i, 0)),
                  pl.BlockSpec((tf, LANES), lambda i, dst: (i, 0)),
                  pl.BlockSpec((1, 1, d), lambda i, dst: (i // bpt, 0, 0)),
                  pl.BlockSpec((1, d), lambda i, dst: (0, 0))],
        out_specs=pl.BlockSpec((tf, d), lambda i, dst: (i, 0)),
        scratch_shapes=[pltpu.VMEM((TOP_K_INNER, tf, d), F32)] * (GATHER_LOOKAHEAD + 1) + [
                        pltpu.SemaphoreType.DMA((GATHER_LOOKAHEAD + 1,))],
    )
    return pl.pallas_call(
        functools.partial(_final_kernel, last_layer=last_layer),
        grid_spec=grid_spec,
        out_shape=jax.ShapeDtypeStruct((t, d), F32),
        compiler_params=_cparams(("arbitrary",)),
        name="final",
    )(dest, yr, x1, rt, gt2, final_g)


def _dispatch_tables(rt, counts_row):
    t = rt.shape[0]
    tk = t * TOP_K_INNER
    eid = rt[:, RT_E0:RT_E1 + 1].astype(jnp.int32)
    rank = rt[:, RT_R0:RT_R1 + 1].astype(jnp.int32)
    counts = counts_row[:N_EXPERTS].astype(jnp.int32)
    padded = (counts + MOE_BLOCK - 1) // MOE_BLOCK * MOE_BLOCK
    pad_end = jnp.cumsum(padded)
    pad_start = pad_end - padded
    dest = (pad_start[eid] + rank).reshape(tk)
    r = tk + N_EXPERTS * MOE_BLOCK
    tok = jnp.repeat(jnp.arange(t, dtype=jnp.int32), TOP_K_INNER)
    row_tok = jnp.zeros((r,), jnp.int32).at[dest].set(tok)
    nblk = r // MOE_BLOCK
    blk_start = jnp.arange(nblk, dtype=jnp.int32) * MOE_BLOCK
    blk_e = jnp.minimum(jnp.sum(pad_end[None, :] <= blk_start[:, None], axis=1), N_EXPERTS - 1).astype(jnp.int32)
    nused = (pad_end[-1:] // MOE_BLOCK).astype(jnp.int32)
    return dest.astype(jnp.int32), row_tok, blk_e, nused


def kernel(x, c, norm1_g, norm2_g, final_g, w_ada, b_ada, w_in, r_lower, r_norm_g,
           w_up_a, w_up_r, w_out, w_rg, b_rg, w_re, b_re, w1, w3, w2):
    batch, seq, d = x.shape
    t = batch * seq
    depth = w_in.shape[0]
    x2 = x.reshape(t, d)
    c_pad = jnp.zeros((8, d), F32).at[:batch].set(c)
    for l in range(depth):
        mod = _ada(c_pad, w_ada[l], b_ada[l][None, :])[:batch]
        sh1, sc1, gt1, sh2, sc2, gt2 = [m[:, None, :] for m in jnp.split(mod, 6, axis=-1)]
        proj = _proj(x2, norm1_g[l][None, :], sc1, sh1, w_in[l], seq)
        att = _moba(proj, batch, seq)
        o_r = _hgrn(proj, r_lower, r_norm_g[l][None, :], batch, seq, l)
        w_router = jnp.zeros((d, LANES), F32).at[:, :N_GROUPS].set(w_rg[l]) \
            .at[:, N_GROUPS:N_GROUPS + N_EXPERTS].set(w_re[l])
        b_router = jnp.zeros((1, LANES), F32).at[0, :N_GROUPS].set(b_rg[l]) \
            .at[0, N_GROUPS:N_GROUPS + N_EXPERTS].set(b_re[l])
        w_router_top = _bf16_part(w_router)
        w_router = jnp.concatenate([w_router_top, w_router - w_router_top], axis=-1).astype(BF16)
        x1, h2, logits = _merge(att, o_r, proj, x2, gt1, w_up_a[l].astype(BF16), w_up_r[l].astype(BF16),
                                w_out[l].astype(BF16), norm2_g[l][None, :], sc2, sh2, w_router, b_router, seq)
        rt, cnt = _route(logits)
        dest, row_tok, blk_e, nused = _dispatch_tables(rt, cnt[0])
        yr = _moe(row_tok, blk_e, nused, h2, w1[l], w3[l], w2[l])
        x2 = _final(dest, yr, x1, rt, gt2, final_g[None, :], seq, l == depth - 1)
    return x2.reshape(batch, seq, d)
```

```python
import functools

import jax
import jax.numpy as jnp
from jax import lax
from jax.experimental import pallas as pl
from jax.experimental.pallas import tpu as pltpu

F32 = jnp.float32
BF16 = jnp.bfloat16
HIGHEST = lax.Precision.HIGHEST

D_MODEL = 2048
A_HEADS = 8
A_HEAD_DIM = 128
A_WIDTH = A_HEADS * A_HEAD_DIM
MOBA_BLOCK = 256
MOBA_TOPK = 3
R_HEADS = 8
R_KEY_DIM = 128
R_VAL_DIM = 128
R_WIDTH = R_HEADS * R_KEY_DIM
N_GROUPS = 4
EXPERTS_PER_GROUP = 8
N_EXPERTS = N_GROUPS * EXPERTS_PER_GROUP
TOP_K_INNER = 2
D_EXPERT = 512
MOE_BLOCK = 256
EPS = 1e-6
IN_COLS = 3 * A_WIDTH + 4 * R_WIDTH + 2 * D_MODEL

LANES = 128
SUBLANES = 8
CB_AQ = 0
CB_AK = CB_AQ + A_WIDTH // LANES
CB_AV = CB_AK + A_WIDTH // LANES
CB_RQ = CB_AV + A_WIDTH // LANES
CB_RF = CB_RQ + R_WIDTH // LANES
CB_RI = CB_RF + R_WIDTH // LANES
CB_ROG = CB_RI + R_WIDTH // LANES
CB_GA = CB_ROG + R_WIDTH // LANES
CB_GR = CB_GA + D_MODEL // LANES

MOBA_GROUP = 4
Q_PRESCALE = A_HEAD_DIM ** -0.5 * 1.4426950408889634
HGRN_TILE = 512
HGRN_HEADS_PER_STEP = 2
HGRN_CHUNK = 64
HGRN_SUB = 8
GATHER_LOOKAHEAD = 2
NEG_BIG = -1e30
VMEM_LIMIT = 56 * 1024 * 1024


def _cparams(sem):
    return pltpu.CompilerParams(dimension_semantics=sem, vmem_limit_bytes=VMEM_LIMIT)


def _dot_nt(a, b, **kw):
    return lax.dot_general(a, b, (((1,), (1,)), ((), ())), preferred_element_type=F32, **kw)


def _bf16_part(a):
    bits = lax.bitcast_convert_type(a, jnp.uint32) & jnp.uint32(0xFFFF0000)
    return lax.bitcast_convert_type(bits, F32)


def _dot_tn(a, b):
    return lax.dot_general(a, b, (((0,), (0,)), ((), ())), preferred_element_type=F32)


def _ada_kernel(c_ref, w_ref, b_ref, o_ref):
    c = c_ref[...]
    ca = c * jax.nn.sigmoid(c)
    w = w_ref[...]
    ca_top, w_top = _bf16_part(ca), _bf16_part(w)
    ca_hi, ca_lo = ca_top.astype(BF16), (ca - ca_top).astype(BF16)
    w_hi, w_lo = w_top.astype(BF16), (w - w_top).astype(BF16)
    acc = jnp.dot(ca_hi, w_hi, preferred_element_type=F32) + jnp.dot(ca_lo, w_hi, preferred_element_type=F32)
    o_ref[...] = acc + jnp.dot(ca_hi, w_lo, preferred_element_type=F32) + b_ref[...]


def _ada(c_pad, w_ada, b_ada):
    rows, d = c_pad.shape
    n = w_ada.shape[1]
    tn = 1024
    return pl.pallas_call(
        _ada_kernel,
        grid=(n // tn,),
        in_specs=[pl.BlockSpec((rows, d), lambda j: (0, 0)),
                  pl.BlockSpec((d, tn), lambda j: (0, j)),
                  pl.BlockSpec((1, tn), lambda j: (0, j))],
        out_specs=pl.BlockSpec((rows, tn), lambda j: (0, j)),
        out_shape=jax.ShapeDtypeStruct((rows, n), F32),
        compiler_params=_cparams(("arbitrary",)),
        name="ada",
    )(c_pad, w_ada, b_ada)


def _proj_kernel(x_ref, g_ref, sc_ref, sh_ref, w_ref, cs_ref, o_ref, h_scr):
    @pl.when(pl.program_id(1) == 0)
    def _():
        x = x_ref[...]
        ms = jnp.mean(x * x, axis=-1, keepdims=True)
        y = x * lax.rsqrt(ms + EPS) * g_ref[...]
        h_scr[...] = (y * (1.0 + sc_ref[0]) + sh_ref[0]).astype(BF16)

    acc = jnp.dot(h_scr[...], w_ref[...].astype(BF16), preferred_element_type=F32)
    o_ref[...] = (acc * cs_ref[...]).astype(o_ref.dtype)


def _proj(x2, g, sc, sh, w_in, seq):
    t, d = x2.shape
    n = w_in.shape[1]
    tm = min(1024, seq)
    tn = 1024
    bpt = seq // tm
    col_scale = jnp.ones((1, n), F32).at[:, CB_AQ * LANES:CB_AK * LANES].set(Q_PRESCALE)
    return pl.pallas_call(
        _proj_kernel,
        grid=(t // tm, n // tn),
        in_specs=[pl.BlockSpec((tm, d), lambda i, j: (i, 0)),
                  pl.BlockSpec((1, d), lambda i, j: (0, 0)),
                  pl.BlockSpec((1, 1, d), lambda i, j: (i // bpt, 0, 0)),
                  pl.BlockSpec((1, 1, d), lambda i, j: (i // bpt, 0, 0)),
                  pl.BlockSpec((d, tn), lambda i, j: (0, j)),
                  pl.BlockSpec((1, tn), lambda i, j: (0, j))],
        out_specs=pl.BlockSpec((tm, tn), lambda i, j: (i, j)),
        out_shape=jax.ShapeDtypeStruct((t, n), BF16),
        scratch_shapes=[pltpu.VMEM((tm, d), BF16)],
        compiler_params=_cparams(("arbitrary", "arbitrary")),
        name="proj",
    )(x2, g, sc, sh, w_in, col_scale)


def _moba_kernel(q_ref, k_ref, v_ref, o_ref, kaug_scr, vt_scr, kmean_scr, s_scr, *, nb, gr):
    blk = MOBA_BLOCK
    gk = MOBA_GROUP * blk
    seq = k_ref.shape[0]

    k = k_ref[...]
    kaug_scr[:, :LANES] = k
    row_blk = lax.broadcasted_iota(jnp.int32, (seq, LANES), 0) // blk
    lane = lax.broadcasted_iota(jnp.int32, (seq, LANES), 1)
    kaug_scr[:, LANES:] = jnp.where(lane == row_blk, 1.0, 0.0).astype(BF16)
    kmean_scr[...] = jnp.zeros_like(kmean_scr)
    kmean_scr[0:nb, :] = jnp.mean(k.astype(F32).reshape(nb, blk, LANES), axis=1)
    for j in range(nb):
        vj = v_ref[j * blk:(j + 1) * blk, :].astype(F32)
        vt_scr[0:LANES, j * blk:(j + 1) * blk] = vj.T.astype(BF16)
    vt_scr[LANES:LANES + 16, :] = jnp.where(
        lax.broadcasted_iota(jnp.int32, (16, seq), 0) == 0, 1.0, 0.0).astype(BF16)

    rid = lax.broadcasted_iota(jnp.int32, (gr, blk), 0)

    def scores(i):
        q = q_ref[i * blk:(i + 1) * blk, :]
        gate = _dot_nt(kmean_scr[...], q.astype(F32), precision=HIGHEST)
        gate = jnp.where(rid < i, gate, -jnp.inf)
        sel = rid == i
        for _ in range(MOBA_TOPK):
            m = jnp.max(gate, axis=0, keepdims=True)
            idx = jnp.min(jnp.where(gate == m, rid, gr), axis=0, keepdims=True)
            pick = rid == idx
            sel = sel | (pick & (rid < i))
            gate = jnp.where(pick, -jnp.inf, gate)
        bias_t = jnp.where(sel, 0.0, NEG_BIG)
        bias_t = jnp.concatenate([bias_t, jnp.zeros((LANES - gr, blk), F32)], axis=0)
        q_aug = jnp.concatenate([q, bias_t.T.astype(BF16)], axis=-1)
        n = (i + 1) * blk
        spans = [(r0, min(r0 + gk, n)) for r0 in range(0, n, gk)]
        slot = i % 2
        mx = jnp.full((8, blk), NEG_BIG, F32)
        for r0, r1 in spans:
            s = _dot_nt(kaug_scr[r0:r1, :], q_aug)
            if r1 == n:
                kpos = r0 + lax.broadcasted_iota(jnp.int32, (r1 - r0, blk), 0)
                qpos = i * blk + lax.broadcasted_iota(jnp.int32, (r1 - r0, blk), 1)
                s = jnp.where(kpos <= qpos, s, NEG_BIG)
            s_scr[slot, r0:r1, :] = s
            mx = jnp.maximum(mx, jnp.max(s.reshape((r1 - r0) // 8, 8, blk), axis=0))
        return jnp.max(mx, axis=0, keepdims=True), spans

    def weighted_values(i, m, spans):
        slot = i % 2
        acc = jnp.zeros((LANES + 16, blk), F32)
        for r0, r1 in spans:
            p = jnp.exp2(s_scr[slot, r0:r1, :] - m).astype(BF16)
            acc = acc + jnp.dot(vt_scr[:, r0:r1], p, preferred_element_type=F32)
        out_t = acc[0:LANES] / acc[LANES:LANES + 1]
        o_ref[i * blk:(i + 1) * blk, :] = out_t.T.astype(o_ref.dtype)

    pending = scores(0)
    for i in range(nb):
        nxt = scores(i + 1) if i + 1 < nb else None
        weighted_values(i, *pending)
        pending = nxt


def _moba(proj, batch, seq):
    nb = seq // MOBA_BLOCK
    blk = MOBA_BLOCK
    gr = -(-nb // 8) * 8
    t = batch * seq
    assert gr <= LANES
    return pl.pallas_call(
        functools.partial(_moba_kernel, nb=nb, gr=gr),
        grid=(batch, A_HEADS),
        in_specs=[pl.BlockSpec((seq, LANES), lambda b, h: (b, CB_AQ + h)),
                  pl.BlockSpec((seq, LANES), lambda b, h: (b, CB_AK + h)),
                  pl.BlockSpec((seq, LANES), lambda b, h: (b, CB_AV + h))],
        out_specs=pl.BlockSpec((seq, LANES), lambda b, h: (b, h)),
        out_shape=jax.ShapeDtypeStruct((t, A_WIDTH), BF16),
        scratch_shapes=[pltpu.VMEM((seq, 2 * LANES), BF16),
                        pltpu.VMEM((LANES + 16, seq), BF16),
                        pltpu.VMEM((gr, LANES), F32),
                        pltpu.VMEM((2, seq, blk), F32)],
        compiler_params=_cparams(("arbitrary", "arbitrary")),
        name="moba",
    )(proj, proj, proj)


def _hgrn_kernel(q_ref, f_ref, i_ref, og_ref, rl_ref, g_ref, o_ref, st_scr, b_scr, *, layer):
    @pl.when(pl.program_id(2) == 0)
    def _():
        st_scr[...] = jnp.zeros_like(st_scr)

    for hh in range(q_ref.shape[1] // LANES):
        cs = slice(hh * LANES, (hh + 1) * LANES)
        _hgrn_head(q_ref.at[:, cs], f_ref.at[:, cs], i_ref.at[:, cs], og_ref.at[:, cs], rl_ref.at[:, cs],
                   g_ref, o_ref.at[:, cs], st_scr.at[hh], b_scr.at[hh], layer=layer)


def _hgrn_head(q_ref, f_ref, i_ref, og_ref, rl_ref, g_ref, o_ref, st_scr, b_scr, *, layer):
    tt = q_ref.shape[0]
    ch, sub = HGRN_CHUNK, HGRN_SUB
    ns, nc = ch // sub, tt // ch
    assert SUBLANES % sub == 0 and ch % SUBLANES == 0 and tt % ch == 0

    rl = rl_ref[...]
    e = jnp.exp(rl - jnp.max(rl, axis=0, keepdims=True))
    lb = jnp.sum(e[: layer + 1], axis=0, keepdims=True) / jnp.sum(e, axis=0, keepdims=True)

    q = q_ref[...].astype(F32)
    x = f_ref[...].astype(F32)
    v_bf = i_ref[...]
    v = v_bf.astype(F32)
    f = lb + (1.0 - lb) * jax.nn.sigmoid(x)
    kin = (1.0 - lb) * jax.nn.sigmoid(-x)
    row = lax.broadcasted_iota(jnp.int32, (tt, LANES), 0)
    pic = row % ch
    pos = row % sub

    def roll8(a, k):
        return pltpu.roll(a.reshape(tt // SUBLANES, SUBLANES, LANES), k, 1).reshape(tt, LANES)

    b = jnp.log2(f)
    pos8 = row % SUBLANES
    step = 1
    while step < SUBLANES:
        b = b + jnp.where(pos8 >= step, roll8(b, step), 0.0)
        step *= 2
    gpc = ch // SUBLANES
    b4 = b.reshape(nc, gpc, SUBLANES, LANES)
    offs = [jnp.zeros((nc, 1, 1, LANES), F32)]
    for g in range(1, gpc):
        offs.append(offs[-1] + b4[:, g - 1:g, SUBLANES - 1:SUBLANES, :])
    b = (b4 + jnp.concatenate(offs, axis=1)).reshape(tt, LANES)
    b_scr[...] = b

    r_sub = b
    for k in range(1, sub):
        r_sub = jnp.where(pos == sub - 1 - k, roll8(b, SUBLANES - k), r_sub)
    kh = kin * jnp.exp2(r_sub - b)

    diag = jnp.sum(q * kin, axis=-1, keepdims=True) * v
    for delta in range(1, sub):
        kr = roll8(kin, delta)
        br = roll8(b, delta)
        vr = roll8(v, delta)
        w = jnp.where(pos >= delta, q * kr * jnp.exp2(b - br), 0.0)
        diag = diag + jnp.sum(w, axis=-1, keepdims=True) * vr

    pc = lax.broadcasted_iota(jnp.int32, (ch, LANES), 0)
    sid = pc // sub
    st = st_scr[...]
    outs = []
    for c in range(nc):
        sl = slice(c * ch, (c + 1) * ch)
        q_c, b_c, kh_c, vc = q[sl], b[sl], kh[sl], v_bf[sl]
        q_blocks, k_blocks = [], []
        for j in range(ns - 1):
            rj = b_scr[pl.ds(c * ch + sub * (j + 1) - 1, 1), :]
            qj = q_c * jnp.exp2(jnp.where(pc >= sub * (j + 1), b_c - rj, -jnp.inf))
            q_blocks.append(qj.astype(BF16))
            k_blocks.append(jnp.where(sid == j, kh_c, 0.0).astype(BF16))
        a_off = _dot_nt(jnp.concatenate(q_blocks, axis=-1), jnp.concatenate(k_blocks, axis=-1))
        o_c = jnp.dot(a_off.astype(BF16), vc, preferred_element_type=F32)
        o_c = o_c + _dot_nt((q_c * jnp.exp2(b_c)).astype(BF16), st.astype(BF16)) + diag[sl]
        r_last = b_scr[pl.ds((c + 1) * ch - 1, 1), :]
        ke = (kh_c * jnp.exp2(r_last - r_sub[sl])).astype(BF16)
        st = jnp.exp2(r_last) * st + _dot_tn(vc, ke)
        outs.append(o_c)
    st_scr[...] = st
    o = jnp.concatenate(outs, axis=0)

    y = o * lax.rsqrt(jnp.mean(o * o, axis=-1, keepdims=True) + EPS) * g_ref[...]
    og = og_ref[...].astype(F32)
    o_ref[...] = (y * (og * jax.nn.sigmoid(og))).astype(o_ref.dtype)


def _hgrn(proj, r_lower, r_norm_g, batch, seq, layer):
    tt = min(HGRN_TILE, seq)
    nt = seq // tt
    t = batch * seq
    nl = r_lower.shape[0]

    nh = HGRN_HEADS_PER_STEP
    hw = nh * LANES
    assert R_HEADS % nh == 0 and all(cb % nh == 0 for cb in (CB_RQ, CB_RF, CB_RI, CB_ROG))

    def col(cb):
        return pl.BlockSpec((tt, hw), lambda b, h, c: (b * nt + c, cb // nh + h))

    return pl.pallas_call(
        functools.partial(_hgrn_kernel, layer=layer),
        grid=(batch, R_HEADS // nh, nt),
        in_specs=[col(CB_RQ), col(CB_RF), col(CB_RI), col(CB_ROG),
                  pl.BlockSpec((nl, hw), lambda b, h, c: (0, h)),
                  pl.BlockSpec((1, LANES), lambda b, h, c: (0, 0))],
        out_specs=pl.BlockSpec((tt, hw), lambda b, h, c: (b * nt + c, h)),
        out_shape=jax.ShapeDtypeStruct((t, R_WIDTH), BF16),
        scratch_shapes=[pltpu.VMEM((nh, LANES, LANES), F32), pltpu.VMEM((nh, tt, LANES), F32)],
        compiler_params=_cparams(("arbitrary", "arbitrary", "arbitrary")),
        name="hgrn",
    )(proj, proj, proj, proj, r_lower, r_norm_g)


def _merge_kernel(att_ref, or_ref, ga0_ref, ga1_ref, gr0_ref, gr1_ref, x_ref, gt_ref,
                  wua_ref, wur_ref, wo_ref, g2_ref, sc_ref, sh_ref, wr_ref, br_ref,
                  x1_ref, h2_ref, lg_ref):
    ya = jnp.dot(att_ref[...], wua_ref[...], preferred_element_type=F32)
    yr = jnp.dot(or_ref[...], wur_ref[...], preferred_element_type=F32)
    ga = jnp.concatenate([ga0_ref[...], ga1_ref[...]], axis=-1).astype(F32)
    gr = jnp.concatenate([gr0_ref[...], gr1_ref[...]], axis=-1).astype(F32)
    merged = jax.nn.sigmoid(ga) * ya + jax.nn.sigmoid(gr) * yr
    out = jnp.dot(merged.astype(BF16), wo_ref[...], preferred_element_type=F32)
    x1 = x_ref[...] + gt_ref[0] * out
    x1_ref[...] = x1
    ms = jnp.mean(x1 * x1, axis=-1, keepdims=True)
    h2 = (x1 * lax.rsqrt(ms + EPS) * g2_ref[...]) * (1.0 + sc_ref[0]) + sh_ref[0]
    h2_ref[...] = h2
    h2_top = _bf16_part(h2)
    h2_hi = h2_top.astype(BF16)
    h2_lo = (h2 - h2_top).astype(BF16)
    wr = wr_ref[...]
    p_hi = jnp.dot(h2_hi, wr, preferred_element_type=F32)
    p_lo = jnp.dot(h2_lo, wr[:, :LANES], preferred_element_type=F32)
    lg_ref[...] = p_hi[:, :LANES] + p_hi[:, LANES:] + p_lo + br_ref[...]


def _merge(att, o_r, proj, x2, gt1, w_up_a, w_up_r, w_out, g2, sc2, sh2, w_router, b_router, seq):
    t, d = x2.shape
    tm = 256
    bpt = seq // tm
    half = d // 2
    cb = half // LANES

    def gspec(cb0, k):
        return pl.BlockSpec((tm, half), lambda i: (i, cb0 // cb + k))

    def const(shape):
        return pl.BlockSpec(shape, lambda i: tuple(0 for _ in shape), pipeline_mode=pl.Buffered(1))

    def per_batch():
        return pl.BlockSpec((1, 1, d), lambda i: (i // bpt, 0, 0))

    return pl.pallas_call(
        _merge_kernel,
        grid=(t // tm,),
        in_specs=[pl.BlockSpec((tm, A_WIDTH), lambda i: (i, 0)),
                  pl.BlockSpec((tm, R_WIDTH), lambda i: (i, 0)),
                  gspec(CB_GA, 0), gspec(CB_GA, 1), gspec(CB_GR, 0), gspec(CB_GR, 1),
                  pl.BlockSpec((tm, d), lambda i: (i, 0)),
                  per_batch(),
                  const((A_WIDTH, d)), const((R_WIDTH, d)), const((d, d)),
                  const((1, d)), per_batch(), per_batch(),
                  const((d, 2 * LANES)), const((1, LANES))],
        out_specs=[pl.BlockSpec((tm, d), lambda i: (i, 0)),
                   pl.BlockSpec((tm, d), lambda i: (i, 0)),
                   pl.BlockSpec((tm, LANES), lambda i: (i, 0))],
        out_shape=[jax.ShapeDtypeStruct((t, d), F32),
                   jax.ShapeDtypeStruct((t, d), F32),
                   jax.ShapeDtypeStruct((t, LANES), F32)],
        compiler_params=_cparams(("arbitrary",)),
        name="merge",
    )(att, o_r, proj, proj, proj, proj, x2, gt1, w_up_a, w_up_r, w_out, g2, sc2, sh2, w_router, b_router)


RT_E0, RT_E1, RT_W0, RT_W1, RT_R0, RT_R1 = 0, 1, 2, 3, 4, 5


def _route_kernel(lg_ref, rt_ref, cnt_ref, carry_scr):
    tr = lg_ref.shape[0]

    @pl.when(pl.program_id(0) == 0)
    def _():
        carry_scr[...] = jnp.zeros_like(carry_scr)

    x = lg_ref[...]
    lane = lax.broadcasted_iota(jnp.int32, (tr, LANES), 1)
    ninf = -jnp.inf

    def lane_max(val):
        return jnp.max(val, axis=-1, keepdims=True)

    def first_lane(mask):
        return jnp.min(jnp.where(mask, lane, LANES), axis=-1, keepdims=True)

    is_g = lane < N_GROUPS
    gmax = lane_max(jnp.where(is_g, x, ninf))
    grp = first_lane(is_g & (x == gmax))
    eg = jnp.where(is_g, jnp.exp(x - gmax), 0.0)
    pg_top = 1.0 / jnp.sum(eg, axis=-1, keepdims=True)

    lo = N_GROUPS + grp * EXPERTS_PER_GROUP
    is_e = (lane >= lo) & (lane < lo + EXPERTS_PER_GROUP)
    emax = lane_max(jnp.where(is_e, x, ninf))
    ee = jnp.where(is_e, jnp.exp(x - emax), 0.0)
    pe = ee / jnp.sum(ee, axis=-1, keepdims=True)
    pe = jnp.where(is_e, pe, ninf)
    p0 = lane_max(pe)
    l0 = first_lane(pe == p0)
    pe1 = jnp.where(lane == l0, ninf, pe)
    p1 = lane_max(pe1)
    l1 = first_lane(pe1 == p1)
    den = p0 + p1
    w0 = pg_top * p0 / den
    w1 = pg_top * p1 / den
    e0 = l0 - N_GROUPS
    e1 = l1 - N_GROUPS

    onehot = ((lane == e0) | (lane == e1)).astype(BF16)
    r = lax.broadcasted_iota(jnp.int32, (tr, tr), 0)
    c = lax.broadcasted_iota(jnp.int32, (tr, tr), 1)
    tri = jnp.where(c < r, 1.0, 0.0).astype(BF16)
    prefix = jnp.dot(tri, onehot, preferred_element_type=F32) + carry_scr[0:1, :]
    rank0 = jnp.sum(jnp.where(lane == e0, prefix, 0.0), axis=-1, keepdims=True)
    rank1 = jnp.sum(jnp.where(lane == e1, prefix, 0.0), axis=-1, keepdims=True)
    total = carry_scr[0:1, :] + jnp.sum(onehot.astype(F32), axis=0, keepdims=True)
    carry_scr[...] = jnp.broadcast_to(total, carry_scr.shape)
    cnt_ref[...] = jnp.broadcast_to(total, cnt_ref.shape)

    rec = jnp.zeros((tr, LANES), F32)
    for k, val in ((RT_E0, e0.astype(F32)), (RT_E1, e1.astype(F32)), (RT_W0, w0), (RT_W1, w1),
                   (RT_R0, rank0), (RT_R1, rank1)):
        rec = jnp.where(lane == k, val, rec)
    rt_ref[...] = rec


def _route(logits):
    t = logits.shape[0]
    tr = 512
    return pl.pallas_call(
        _route_kernel,
        grid=(t // tr,),
        in_specs=[pl.BlockSpec((tr, LANES), lambda i: (i, 0))],
        out_specs=[pl.BlockSpec((tr, LANES), lambda i: (i, 0)),
                   pl.BlockSpec((8, LANES), lambda i: (0, 0))],
        out_shape=[jax.ShapeDtypeStruct((t, LANES), F32),
                   jax.ShapeDtypeStruct((8, LANES), F32)],
        scratch_shapes=[pltpu.VMEM((8, LANES), F32)],
        compiler_params=_cparams(("arbitrary",)),
        name="route",
    )(logits)


def _row_copy(src_hbm, dst_buf, src_row, dst_row, sem):
    return pltpu.make_async_copy(src_hbm.at[pl.ds(src_row, 1), :], dst_buf.at[pl.ds(dst_row, 1), :], sem)


def _wait_rows(src_hbm, dst_buf, n_rows, sem):
    def body(r, _):
        _row_copy(src_hbm, dst_buf, 0, r, sem).wait()
        return 0

    lax.fori_loop(0, n_rows, body, 0, unroll=8)


def _moe_kernel(tok_ref, be_ref, nu_ref, run_ref, nxt_ref, h_hbm, w1_hbm, w3_hbm, w2_hbm, y_ref,
                xbuf_a, xbuf_b, xbuf_c, sem, w1_f, w3_f, w2_f, wsem, w1_bf, w3_bf, w2_bf):
    rows = MOE_BLOCK
    i = pl.program_id(0)
    nused = nu_ref[0]
    bufs = (xbuf_a, xbuf_b, xbuf_c)
    depth = GATHER_LOOKAHEAD

    def issue(block, parity):
        base = block * rows
        for r in range(rows):
            _row_copy(h_hbm, bufs[parity], tok_ref[base + r], r, sem.at[parity]).start()

    for first in range(depth):
        @pl.when((i == 0) & (first < nused))
        def _(first=first):
            issue(first, first)

    def weight_copies(expert, slot):
        return [pltpu.make_async_copy(src.at[expert], dst.at[slot], wsem.at[slot])
                for src, dst in ((w1_hbm, w1_f), (w3_hbm, w3_f), (w2_hbm, w2_f))]

    new_expert = (i == 0) | (be_ref[i] != be_ref[jnp.maximum(i - 1, 0)])
    wslot = run_ref[i] % 2

    @pl.when((i == 0) & (nused > 0))
    def _():
        for cp in weight_copies(be_ref[0], 0):
            cp.start()

    @pl.when((i < nused) & new_expert)
    def _():
        for cp in weight_copies(be_ref[i], wslot):
            cp.wait()

        @pl.when(nxt_ref[i] >= 0)
        def _():
            for cp in weight_copies(nxt_ref[i], 1 - wslot):
                cp.start()

        w1_bf[...] = w1_f[wslot].astype(BF16)
        w3_bf[...] = w3_f[wslot].astype(BF16)
        w2_bf[...] = w2_f[wslot].astype(BF16)

    def block_step(parity, prefetch):
        _wait_rows(h_hbm, bufs[parity], rows, sem.at[parity])
        if prefetch:
            issue(i + depth, (parity + depth) % len(bufs))
        xb = bufs[parity][...].astype(BF16)
        h1 = jnp.dot(xb, w1_bf[...], preferred_element_type=F32)
        h3 = jnp.dot(xb, w3_bf[...], preferred_element_type=F32)
        hid = (h1 * jax.nn.sigmoid(h1)) * h3
        y_ref[...] = jnp.dot(hid.astype(BF16), w2_bf[...], preferred_element_type=F32)

    for parity in range(len(bufs)):
        for prefetch in (True, False):
            more = (i + depth < nused) if prefetch else ((i < nused) & (i + depth >= nused))
            pl.when((i % len(bufs) == parity) & more)(functools.partial(block_step, parity, prefetch))

    @pl.when(i >= nused)
    def _():
        y_ref[...] = jnp.zeros_like(y_ref)


def _moe(row_tok, blk_e, nused, run_id, next_e, h2, w1, w3, w2):
    t, d = h2.shape
    r = row_tok.shape[0]
    nblk = r // MOE_BLOCK
    grid_spec = pltpu.PrefetchScalarGridSpec(
        num_scalar_prefetch=5,
        grid=(nblk,),
        in_specs=[pl.BlockSpec(memory_space=pl.ANY)] * 4,
        out_specs=pl.BlockSpec((MOE_BLOCK, d), lambda i, *_: (i, 0)),
        scratch_shapes=[pltpu.VMEM((MOE_BLOCK, d), F32)] * (GATHER_LOOKAHEAD + 1) + [
                        pltpu.SemaphoreType.DMA((GATHER_LOOKAHEAD + 1,)),
                        pltpu.VMEM((2, d, D_EXPERT), F32), pltpu.VMEM((2, d, D_EXPERT), F32),
                        pltpu.VMEM((2, D_EXPERT, d), F32), pltpu.SemaphoreType.DMA((2,)),
                        pltpu.VMEM((d, D_EXPERT), BF16), pltpu.VMEM((d, D_EXPERT), BF16),
                        pltpu.VMEM((D_EXPERT, d), BF16)],
    )
    return pl.pallas_call(
        _moe_kernel,
        grid_spec=grid_spec,
        out_shape=jax.ShapeDtypeStruct((r, d), F32),
        compiler_params=_cparams(("arbitrary",)),
        name="moe",
    )(row_tok, blk_e, nused, run_id, next_e, h2, w1, w3, w2)


def _final_kernel(dest_ref, y_hbm, x1_ref, rt_ref, gt_ref, g_ref, o_ref, ybuf_a, ybuf_b, ybuf_c, sem, *,
                  last_layer):
    tf = x1_ref.shape[0]
    i = pl.program_id(0)
    n = pl.num_programs(0)
    bufs = (ybuf_a, ybuf_b, ybuf_c)
    depth = GATHER_LOOKAHEAD

    def issue(step, parity):
        base = step * tf * TOP_K_INNER
        for r in range(tf):
            for k in range(TOP_K_INNER):
                _row_copy(y_hbm, bufs[parity].at[k], dest_ref[base + r * TOP_K_INNER + k], r,
                          sem.at[parity]).start()

    for first in range(depth):
        @pl.when((i == 0) & (first < n))
        def _(first=first):
            issue(first, first)

    def tile_step(parity, prefetch):
        buf = bufs[parity]
        for k in range(TOP_K_INNER):
            _wait_rows(y_hbm, buf.at[k], tf, sem.at[parity])
        if prefetch:
            issue(i + depth, (parity + depth) % len(bufs))
        rt = rt_ref[...]
        w0 = rt[:, RT_W0:RT_W0 + 1]
        w1 = rt[:, RT_W1:RT_W1 + 1]
        y = buf[0] * w0 + buf[1] * w1
        x2 = x1_ref[...] + gt_ref[0] * y
        if last_layer:
            ms = jnp.mean(x2 * x2, axis=-1, keepdims=True)
            x2 = x2 * lax.rsqrt(ms + EPS) * g_ref[...]
        o_ref[...] = x2

    for parity in range(len(bufs)):
        for prefetch in (True, False):
            more = (i + depth < n) if prefetch else (i + depth >= n)
            pl.when((i % len(bufs) == parity) & more)(functools.partial(tile_step, parity, prefetch))


def _final(dest, yr, x1, rt, gt2, final_g, seq, last_layer):
    t, d = x1.shape
    tf = 256
    bpt = seq // tf
    grid_spec = pltpu.PrefetchScalarGridSpec(
        num_scalar_prefetch=1,
        grid=(t // tf,),
        in_specs=[pl.BlockSpec(memory_space=pl.ANY),
                  pl.BlockSpec((tf, d), lambda i, dst: (i, 0)),
                  pl.BlockSpec((tf, LANES), lambda i, dst: (i, 0)),
                  pl.BlockSpec((1, 1, d), lambda i, dst: (i // bpt, 0, 0)),
                  pl.BlockSpec((1, d), lambda i, dst: (0, 0))],
        out_specs=pl.BlockSpec((tf, d), lambda i, dst: (i, 0)),
        scratch_shapes=[pltpu.VMEM((TOP_K_INNER, tf, d), F32)] * (GATHER_LOOKAHEAD + 1) + [
                        pltpu.SemaphoreType.DMA((GATHER_LOOKAHEAD + 1,))],
    )
    return pl.pallas_call(
        functools.partial(_final_kernel, last_layer=last_layer),
        grid_spec=grid_spec,
        out_shape=jax.ShapeDtypeStruct((t, d), F32),
        compiler_params=_cparams(("arbitrary",)),
        name="final",
    )(dest, yr, x1, rt, gt2, final_g)


def _dispatch_tables(rt, counts_row):
    t = rt.shape[0]
    tk = t * TOP_K_INNER
    eid = rt[:, RT_E0:RT_E1 + 1].astype(jnp.int32)
    rank = rt[:, RT_R0:RT_R1 + 1].astype(jnp.int32)
    counts = counts_row[:N_EXPERTS].astype(jnp.int32)
    padded = (counts + MOE_BLOCK - 1) // MOE_BLOCK * MOE_BLOCK
    pad_end = jnp.cumsum(padded)
    pad_start = pad_end - padded
    dest = (pad_start[eid] + rank).reshape(tk).astype(jnp.int32)
    r = tk + N_EXPERTS * MOE_BLOCK
    tok = jnp.repeat(jnp.arange(t, dtype=jnp.int32), TOP_K_INNER)
    row_tok = jnp.zeros((r,), jnp.int32).at[dest].set(tok)
    nblk = r // MOE_BLOCK
    blk_start = jnp.arange(nblk, dtype=jnp.int32) * MOE_BLOCK
    blk_e = jnp.minimum(jnp.sum(pad_end[None, :] <= blk_start[:, None], axis=1), N_EXPERTS - 1).astype(jnp.int32)
    nused = (pad_end[-1:] // MOE_BLOCK).astype(jnp.int32)
    idx = jnp.arange(nblk, dtype=jnp.int32)
    is_new = jnp.concatenate([jnp.ones((1,), bool), blk_e[1:] != blk_e[:-1]])
    run_id = (jnp.cumsum(is_new.astype(jnp.int32)) - 1).astype(jnp.int32)
    later = (idx[None, :] > idx[:, None]) & (blk_e[None, :] != blk_e[:, None]) & (idx[None, :] < nused[0])
    first_later = jnp.min(jnp.where(later, idx[None, :], nblk), axis=1)
    next_e = jnp.where(first_later < nblk, blk_e[jnp.minimum(first_later, nblk - 1)], -1).astype(jnp.int32)
    return dest, row_tok, blk_e, nused, run_id, next_e


def kernel(x, c, norm1_g, norm2_g, final_g, w_ada, b_ada, w_in, r_lower, r_norm_g,
           w_up_a, w_up_r, w_out, w_rg, b_rg, w_re, b_re, w1, w3, w2):
    batch, seq, d = x.shape
    t = batch * seq
    depth = w_in.shape[0]
    x2 = x.reshape(t, d)
    c_pad = jnp.zeros((-(-batch // 16) * 16, d), F32).at[:batch].set(c)
    for l in range(depth):
        mod = _ada(c_pad, w_ada[l], b_ada[l][None, :])[:batch]
        sh1, sc1, gt1, sh2, sc2, gt2 = [m[:, None, :] for m in jnp.split(mod, 6, axis=-1)]
        proj = _proj(x2, norm1_g[l][None, :], sc1, sh1, w_in[l], seq)
        att = _moba(proj, batch, seq)
        o_r = _hgrn(proj, r_lower, r_norm_g[l][None, :], batch, seq, l)
        w_router = jnp.zeros((d, LANES), F32).at[:, :N_GROUPS].set(w_rg[l]) \
            .at[:, N_GROUPS:N_GROUPS + N_EXPERTS].set(w_re[l])
        b_router = jnp.zeros((1, LANES), F32).at[0, :N_GROUPS].set(b_rg[l]) \
            .at[0, N_GROUPS:N_GROUPS + N_EXPERTS].set(b_re[l])
        w_router_top = _bf16_part(w_router)
        w_router = jnp.concatenate([w_router_top, w_router - w_router_top], axis=-1).astype(BF16)
        x1, h2, logits = _merge(att, o_r, proj, x2, gt1, w_up_a[l].astype(BF16), w_up_r[l].astype(BF16),
                                w_out[l].astype(BF16), norm2_g[l][None, :], sc2, sh2, w_router, b_router, seq)
        rt, cnt = _route(logits)
        dest, row_tok, blk_e, nused, run_id, next_e = _dispatch_tables(rt, cnt[0])
        yr = _moe(row_tok, blk_e, nused, run_id, next_e, h2, w1[l], w3[l], w2[l])
        x2 = _final(dest, yr, x1, rt, gt2, final_g[None, :], seq, l == depth - 1)
    return x2.reshape(batch, seq, d)
```

```python
import functools

import jax
import jax.numpy as jnp
from jax import lax
from jax.experimental import pallas as pl
from jax.experimental.pallas import tpu as pltpu

F32 = jnp.float32
BF16 = jnp.bfloat16
HIGHEST = lax.Precision.HIGHEST

D_MODEL = 2048
A_HEADS = 8
A_HEAD_DIM = 128
A_WIDTH = A_HEADS * A_HEAD_DIM
MOBA_BLOCK = 256
MOBA_TOPK = 3
R_HEADS = 8
R_KEY_DIM = 128
R_VAL_DIM = 128
R_WIDTH = R_HEADS * R_KEY_DIM
N_GROUPS = 4
EXPERTS_PER_GROUP = 8
N_EXPERTS = N_GROUPS * EXPERTS_PER_GROUP
TOP_K_INNER = 2
D_EXPERT = 512
MOE_BLOCK = 256
EPS = 1e-6
IN_COLS = 3 * A_WIDTH + 4 * R_WIDTH + 2 * D_MODEL

LANES = 128
SUBLANES = 8
CB_AQ = 0
CB_AK = CB_AQ + A_WIDTH // LANES
CB_AV = CB_AK + A_WIDTH // LANES
CB_RQ = CB_AV + A_WIDTH // LANES
CB_RF = CB_RQ + R_WIDTH // LANES
CB_RI = CB_RF + R_WIDTH // LANES
CB_ROG = CB_RI + R_WIDTH // LANES
CB_GA = CB_ROG + R_WIDTH // LANES
CB_GR = CB_GA + D_MODEL // LANES

MOBA_GROUP = 4
Q_PRESCALE = A_HEAD_DIM ** -0.5 * 1.4426950408889634
HGRN_TILE = 512
HGRN_HEADS_PER_STEP = 4
HGRN_CHUNK = 64
HGRN_SUB = 8
GATHER_LOOKAHEAD = 3
NEG_BIG = -1e30
VMEM_LIMIT = 56 * 1024 * 1024


def _cparams(sem):
    return pltpu.CompilerParams(dimension_semantics=sem, vmem_limit_bytes=VMEM_LIMIT)


def _dot_nt(a, b, **kw):
    return lax.dot_general(a, b, (((1,), (1,)), ((), ())), preferred_element_type=F32, **kw)


def _bf16_part(a):
    bits = lax.bitcast_convert_type(a, jnp.uint32) & jnp.uint32(0xFFFF0000)
    return lax.bitcast_convert_type(bits, F32)


def _dot_tn(a, b):
    return lax.dot_general(a, b, (((0,), (0,)), ((), ())), preferred_element_type=F32)


def _ada_kernel(c_ref, w_ref, b_ref, o_ref):
    c = c_ref[...]
    ca = c * jax.nn.sigmoid(c)
    w = w_ref[...]
    ca_top, w_top = _bf16_part(ca), _bf16_part(w)
    ca_hi, ca_lo = ca_top.astype(BF16), (ca - ca_top).astype(BF16)
    w_hi, w_lo = w_top.astype(BF16), (w - w_top).astype(BF16)
    acc = jnp.dot(ca_hi, w_hi, preferred_element_type=F32) + jnp.dot(ca_lo, w_hi, preferred_element_type=F32)
    o_ref[...] = acc + jnp.dot(ca_hi, w_lo, preferred_element_type=F32) + b_ref[...]


def _ada(c_pad, w_ada, b_ada):
    rows, d = c_pad.shape
    n = w_ada.shape[1]
    tn = 1024
    return pl.pallas_call(
        _ada_kernel,
        grid=(n // tn,),
        in_specs=[pl.BlockSpec((rows, d), lambda j: (0, 0)),
                  pl.BlockSpec((d, tn), lambda j: (0, j)),
                  pl.BlockSpec((1, tn), lambda j: (0, j))],
        out_specs=pl.BlockSpec((rows, tn), lambda j: (0, j)),
        out_shape=jax.ShapeDtypeStruct((rows, n), F32),
        compiler_params=_cparams(("arbitrary",)),
        name="ada",
    )(c_pad, w_ada, b_ada)


def _proj_kernel(x_ref, g_ref, sc_ref, sh_ref, w_ref, cs_ref, o_ref, h_scr):
    @pl.when(pl.program_id(1) == 0)
    def _():
        x = x_ref[...]
        ms = jnp.mean(x * x, axis=-1, keepdims=True)
        y = x * lax.rsqrt(ms + EPS) * g_ref[...]
        h_scr[...] = (y * (1.0 + sc_ref[0]) + sh_ref[0]).astype(BF16)

    acc = jnp.dot(h_scr[...], w_ref[...].astype(BF16), preferred_element_type=F32)
    o_ref[...] = (acc * cs_ref[...]).astype(o_ref.dtype)


def _proj(x2, g, sc, sh, w_in, seq):
    t, d = x2.shape
    n = w_in.shape[1]
    tm = min(1024, seq)
    tn = 1024
    bpt = seq // tm
    col_scale = jnp.ones((1, n), F32).at[:, CB_AQ * LANES:CB_AK * LANES].set(Q_PRESCALE)
    return pl.pallas_call(
        _proj_kernel,
        grid=(t // tm, n // tn),
        in_specs=[pl.BlockSpec((tm, d), lambda i, j: (i, 0)),
                  pl.BlockSpec((1, d), lambda i, j: (0, 0)),
                  pl.BlockSpec((1, 1, d), lambda i, j: (i // bpt, 0, 0)),
                  pl.BlockSpec((1, 1, d), lambda i, j: (i // bpt, 0, 0)),
                  pl.BlockSpec((d, tn), lambda i, j: (0, j)),
                  pl.BlockSpec((1, tn), lambda i, j: (0, j))],
        out_specs=pl.BlockSpec((tm, tn), lambda i, j: (i, j)),
        out_shape=jax.ShapeDtypeStruct((t, n), BF16),
        scratch_shapes=[pltpu.VMEM((tm, d), BF16)],
        compiler_params=_cparams(("arbitrary", "arbitrary")),
        name="proj",
    )(x2, g, sc, sh, w_in, col_scale)


def _moba_kernel(q_ref, k_ref, v_ref, o_ref, kaug_scr, vt_scr, kmean_scr, s_scr, *, nb, gr):
    blk = MOBA_BLOCK
    gk = MOBA_GROUP * blk
    seq = k_ref.shape[0]

    k = k_ref[...]
    kaug_scr[:, :LANES] = k
    row_blk = lax.broadcasted_iota(jnp.int32, (seq, LANES), 0) // blk
    lane = lax.broadcasted_iota(jnp.int32, (seq, LANES), 1)
    kaug_scr[:, LANES:] = jnp.where(lane == row_blk, 1.0, 0.0).astype(BF16)
    kmean_scr[...] = jnp.zeros_like(kmean_scr)
    kmean_scr[0:nb, :] = jnp.mean(k.astype(F32).reshape(nb, blk, LANES), axis=1)
    for j in range(nb):
        vj = v_ref[j * blk:(j + 1) * blk, :].astype(F32)
        vt_scr[0:LANES, j * blk:(j + 1) * blk] = vj.T.astype(BF16)
    vt_scr[LANES:LANES + 16, :] = jnp.where(
        lax.broadcasted_iota(jnp.int32, (16, seq), 0) == 0, 1.0, 0.0).astype(BF16)

    rid = lax.broadcasted_iota(jnp.int32, (gr, blk), 0)

    def scores(i):
        q = q_ref[i * blk:(i + 1) * blk, :]
        gate = _dot_nt(kmean_scr[...], q.astype(F32), precision=HIGHEST)
        gate = jnp.where(rid < i, gate, -jnp.inf)
        sel = rid == i
        for _ in range(MOBA_TOPK):
            m = jnp.max(gate, axis=0, keepdims=True)
            idx = jnp.min(jnp.where(gate == m, rid, gr), axis=0, keepdims=True)
            pick = rid == idx
            sel = sel | (pick & (rid < i))
            gate = jnp.where(pick, -jnp.inf, gate)
        bias_t = jnp.where(sel, 0.0, NEG_BIG)
        bias_t = jnp.concatenate([bias_t, jnp.zeros((LANES - gr, blk), F32)], axis=0)
        q_aug = jnp.concatenate([q, bias_t.T.astype(BF16)], axis=-1)
        n = (i + 1) * blk
        spans = [(r0, min(r0 + gk, n)) for r0 in range(0, n, gk)]
        slot = i % 2
        mx = jnp.full((8, blk), NEG_BIG, F32)
        for r0, r1 in spans:
            s = _dot_nt(kaug_scr[r0:r1, :], q_aug)
            if r1 == n:
                kpos = r0 + lax.broadcasted_iota(jnp.int32, (r1 - r0, blk), 0)
                qpos = i * blk + lax.broadcasted_iota(jnp.int32, (r1 - r0, blk), 1)
                s = jnp.where(kpos <= qpos, s, NEG_BIG)
            s_scr[slot, r0:r1, :] = s
            mx = jnp.maximum(mx, jnp.max(s.reshape((r1 - r0) // 8, 8, blk), axis=0))
        return jnp.max(mx, axis=0, keepdims=True), spans

    def weighted_values(i, m, spans):
        slot = i % 2
        acc = jnp.zeros((LANES + 16, blk), F32)
        for r0, r1 in spans:
            p = jnp.exp2(s_scr[slot, r0:r1, :] - m).astype(BF16)
            acc = acc + jnp.dot(vt_scr[:, r0:r1], p, preferred_element_type=F32)
        out_t = acc[0:LANES] / acc[LANES:LANES + 1]
        o_ref[i * blk:(i + 1) * blk, :] = out_t.T.astype(o_ref.dtype)

    pending = scores(0)
    for i in range(nb):
        nxt = scores(i + 1) if i + 1 < nb else None
        weighted_values(i, *pending)
        pending = nxt


def _moba(proj, batch, seq):
    nb = seq // MOBA_BLOCK
    blk = MOBA_BLOCK
    gr = -(-nb // 8) * 8
    t = batch * seq
    assert gr <= LANES
    return pl.pallas_call(
        functools.partial(_moba_kernel, nb=nb, gr=gr),
        grid=(batch, A_HEADS),
        in_specs=[pl.BlockSpec((seq, LANES), lambda b, h: (b, CB_AQ + h)),
                  pl.BlockSpec((seq, LANES), lambda b, h: (b, CB_AK + h)),
                  pl.BlockSpec((seq, LANES), lambda b, h: (b, CB_AV + h))],
        out_specs=pl.BlockSpec((seq, LANES), lambda b, h: (b, h)),
        out_shape=jax.ShapeDtypeStruct((t, A_WIDTH), BF16),
        scratch_shapes=[pltpu.VMEM((seq, 2 * LANES), BF16),
                        pltpu.VMEM((LANES + 16, seq), BF16),
                        pltpu.VMEM((gr, LANES), F32),
                        pltpu.VMEM((2, seq, blk), F32)],
        compiler_params=_cparams(("arbitrary", "arbitrary")),
        name="moba",
    )(proj, proj, proj)


def _hgrn_kernel(q_ref, f_ref, i_ref, og_ref, rl_ref, g_ref, o_ref, st_scr, b_scr, *, layer):
    @pl.when(pl.program_id(2) == 0)
    def _():
        st_scr[...] = jnp.zeros_like(st_scr)

    for hh in range(q_ref.shape[1] // LANES):
        cs = slice(hh * LANES, (hh + 1) * LANES)
        _hgrn_head(q_ref.at[:, cs], f_ref.at[:, cs], i_ref.at[:, cs], og_ref.at[:, cs], rl_ref.at[:, cs],
                   g_ref, o_ref.at[:, cs], st_scr.at[hh], b_scr.at[hh], layer=layer)


def _hgrn_head(q_ref, f_ref, i_ref, og_ref, rl_ref, g_ref, o_ref, st_scr, b_scr, *, layer):
    tt = q_ref.shape[0]
    ch, sub = HGRN_CHUNK, HGRN_SUB
    ns, nc = ch // sub, tt // ch
    assert SUBLANES % sub == 0 and ch % SUBLANES == 0 and tt % ch == 0

    rl = rl_ref[...]
    e = jnp.exp(rl - jnp.max(rl, axis=0, keepdims=True))
    lb = jnp.sum(e[: layer + 1], axis=0, keepdims=True) / jnp.sum(e, axis=0, keepdims=True)

    q = q_ref[...].astype(F32)
    x = f_ref[...].astype(F32)
    v_bf = i_ref[...]
    v = v_bf.astype(F32)
    f = lb + (1.0 - lb) * jax.nn.sigmoid(x)
    kin = (1.0 - lb) * jax.nn.sigmoid(-x)
    row = lax.broadcasted_iota(jnp.int32, (tt, LANES), 0)
    pic = row % ch
    pos = row % sub

    def roll8(a, k):
        return pltpu.roll(a.reshape(tt // SUBLANES, SUBLANES, LANES), k, 1).reshape(tt, LANES)

    b = jnp.log2(f)
    pos8 = row % SUBLANES
    step = 1
    while step < SUBLANES:
        b = b + jnp.where(pos8 >= step, roll8(b, step), 0.0)
        step *= 2
    gpc = ch // SUBLANES
    b4 = b.reshape(nc, gpc, SUBLANES, LANES)
    offs = [jnp.zeros((nc, 1, 1, LANES), F32)]
    for g in range(1, gpc):
        offs.append(offs[-1] + b4[:, g - 1:g, SUBLANES - 1:SUBLANES, :])
    b = (b4 + jnp.concatenate(offs, axis=1)).reshape(tt, LANES)
    b_scr[...] = b

    r_sub = b
    for k in range(1, sub):
        r_sub = jnp.where(pos == sub - 1 - k, roll8(b, SUBLANES - k), r_sub)
    kh = kin * jnp.exp2(r_sub - b)

    diag = jnp.sum(q * kin, axis=-1, keepdims=True) * v
    for delta in range(1, sub):
        kr = roll8(kin, delta)
        br = roll8(b, delta)
        vr = roll8(v, delta)
        w = jnp.where(pos >= delta, q * kr * jnp.exp2(b - br), 0.0)
        diag = diag + jnp.sum(w, axis=-1, keepdims=True) * vr

    pc = lax.broadcasted_iota(jnp.int32, (ch, LANES), 0)
    sid = pc // sub
    st = st_scr[...]
    outs = []
    for c in range(nc):
        sl = slice(c * ch, (c + 1) * ch)
        q_c, b_c, kh_c, vc = q[sl], b[sl], kh[sl], v_bf[sl]
        q_blocks, k_blocks = [], []
        for j in range(ns - 1):
            rj = b_scr[pl.ds(c * ch + sub * (j + 1) - 1, 1), :]
            qj = q_c * jnp.exp2(jnp.where(pc >= sub * (j + 1), b_c - rj, -jnp.inf))
            q_blocks.append(qj.astype(BF16))
            k_blocks.append(jnp.where(sid == j, kh_c, 0.0).astype(BF16))
        a_off = _dot_nt(jnp.concatenate(q_blocks, axis=-1), jnp.concatenate(k_blocks, axis=-1))
        o_c = jnp.dot(a_off.astype(BF16), vc, preferred_element_type=F32)
        o_c = o_c + _dot_nt((q_c * jnp.exp2(b_c)).astype(BF16), st.astype(BF16)) + diag[sl]
        r_last = b_scr[pl.ds((c + 1) * ch - 1, 1), :]
        ke = (kh_c * jnp.exp2(r_last - r_sub[sl])).astype(BF16)
        st = jnp.exp2(r_last) * st + _dot_tn(vc, ke)
        outs.append(o_c)
    st_scr[...] = st
    o = jnp.concatenate(outs, axis=0)

    y = o * lax.rsqrt(jnp.mean(o * o, axis=-1, keepdims=True) + EPS) * g_ref[...]
    og = og_ref[...].astype(F32)
    o_ref[...] = (y * (og * jax.nn.sigmoid(og))).astype(o_ref.dtype)


def _hgrn(proj, r_lower, r_norm_g, batch, seq, layer):
    tt = min(HGRN_TILE, seq)
    nt = seq // tt
    t = batch * seq
    nl = r_lower.shape[0]

    nh = HGRN_HEADS_PER_STEP
    hw = nh * LANES
    assert R_HEADS % nh == 0 and all(cb % nh == 0 for cb in (CB_RQ, CB_RF, CB_RI, CB_ROG))

    def col(cb):
        return pl.BlockSpec((tt, hw), lambda b, h, c: (b * nt + c, cb // nh + h))

    return pl.pallas_call(
        functools.partial(_hgrn_kernel, layer=layer),
        grid=(batch, R_HEADS // nh, nt),
        in_specs=[col(CB_RQ), col(CB_RF), col(CB_RI), col(CB_ROG),
                  pl.BlockSpec((nl, hw), lambda b, h, c: (0, h)),
                  pl.BlockSpec((1, LANES), lambda b, h, c: (0, 0))],
        out_specs=pl.BlockSpec((tt, hw), lambda b, h, c: (b * nt + c, h)),
        out_shape=jax.ShapeDtypeStruct((t, R_WIDTH), BF16),
        scratch_shapes=[pltpu.VMEM((nh, LANES, LANES), F32), pltpu.VMEM((nh, tt, LANES), F32)],
        compiler_params=_cparams(("arbitrary", "arbitrary", "arbitrary")),
        name="hgrn",
    )(proj, proj, proj, proj, r_lower, r_norm_g)


def _merge_kernel(att_ref, or_ref, ga0_ref, ga1_ref, gr0_ref, gr1_ref, x_ref, gt_ref,
                  wua_ref, wur_ref, wo_ref, g2_ref, sc_ref, sh_ref, wr_ref, br_ref,
                  x1_ref, h2_ref, lg_ref):
    ya = jnp.dot(att_ref[...], wua_ref[...], preferred_element_type=F32)
    yr = jnp.dot(or_ref[...], wur_ref[...], preferred_element_type=F32)
    ga = jnp.concatenate([ga0_ref[...], ga1_ref[...]], axis=-1).astype(F32)
    gr = jnp.concatenate([gr0_ref[...], gr1_ref[...]], axis=-1).astype(F32)
    merged = jax.nn.sigmoid(ga) * ya + jax.nn.sigmoid(gr) * yr
    out = jnp.dot(merged.astype(BF16), wo_ref[...], preferred_element_type=F32)
    x1 = x_ref[...] + gt_ref[0] * out
    x1_ref[...] = x1
    ms = jnp.mean(x1 * x1, axis=-1, keepdims=True)
    h2 = (x1 * lax.rsqrt(ms + EPS) * g2_ref[...]) * (1.0 + sc_ref[0]) + sh_ref[0]
    h2_ref[...] = h2
    h2_top = _bf16_part(h2)
    h2_hi = h2_top.astype(BF16)
    h2_lo = (h2 - h2_top).astype(BF16)
    wr = wr_ref[...]
    p_hi = jnp.dot(h2_hi, wr, preferred_element_type=F32)
    p_lo = jnp.dot(h2_lo, wr[:, :LANES], preferred_element_type=F32)
    lg_ref[...] = p_hi[:, :LANES] + p_hi[:, LANES:] + p_lo + br_ref[...]


def _merge(att, o_r, proj, x2, gt1, w_up_a, w_up_r, w_out, g2, sc2, sh2, w_router, b_router, seq):
    t, d = x2.shape
    tm = 256
    bpt = seq // tm
    half = d // 2
    cb = half // LANES

    def gspec(cb0, k):
        return pl.BlockSpec((tm, half), lambda i: (i, cb0 // cb + k))

    def const(shape):
        return pl.BlockSpec(shape, lambda i: tuple(0 for _ in shape), pipeline_mode=pl.Buffered(1))

    def per_batch():
        return pl.BlockSpec((1, 1, d), lambda i: (i // bpt, 0, 0))

    return pl.pallas_call(
        _merge_kernel,
        grid=(t // tm,),
        in_specs=[pl.BlockSpec((tm, A_WIDTH), lambda i: (i, 0)),
                  pl.BlockSpec((tm, R_WIDTH), lambda i: (i, 0)),
                  gspec(CB_GA, 0), gspec(CB_GA, 1), gspec(CB_GR, 0), gspec(CB_GR, 1),
                  pl.BlockSpec((tm, d), lambda i: (i, 0)),
                  per_batch(),
                  const((A_WIDTH, d)), const((R_WIDTH, d)), const((d, d)),
                  const((1, d)), per_batch(), per_batch(),
                  const((d, 2 * LANES)), const((1, LANES))],
        out_specs=[pl.BlockSpec((tm, d), lambda i: (i, 0)),
                   pl.BlockSpec((tm, d), lambda i: (i, 0)),
                   pl.BlockSpec((tm, LANES), lambda i: (i, 0))],
        out_shape=[jax.ShapeDtypeStruct((t, d), F32),
                   jax.ShapeDtypeStruct((t, d), F32),
                   jax.ShapeDtypeStruct((t, LANES), F32)],
        compiler_params=_cparams(("arbitrary",)),
        name="merge",
    )(att, o_r, proj, proj, proj, proj, x2, gt1, w_up_a, w_up_r, w_out, g2, sc2, sh2, w_router, b_router)


RT_E0, RT_E1, RT_W0, RT_W1, RT_R0, RT_R1 = 0, 1, 2, 3, 4, 5


def _route_kernel(lg_ref, rt_ref, cnt_ref, carry_scr):
    tr = lg_ref.shape[0]

    @pl.when(pl.program_id(0) == 0)
    def _():
        carry_scr[...] = jnp.zeros_like(carry_scr)

    x = lg_ref[...]
    lane = lax.broadcasted_iota(jnp.int32, (tr, LANES), 1)
    ninf = -jnp.inf

    def lane_max(val):
        return jnp.max(val, axis=-1, keepdims=True)

    def first_lane(mask):
        return jnp.min(jnp.where(mask, lane, LANES), axis=-1, keepdims=True)

    is_g = lane < N_GROUPS
    gmax = lane_max(jnp.where(is_g, x, ninf))
    grp = first_lane(is_g & (x == gmax))
    eg = jnp.where(is_g, jnp.exp(x - gmax), 0.0)
    pg_top = 1.0 / jnp.sum(eg, axis=-1, keepdims=True)

    lo = N_GROUPS + grp * EXPERTS_PER_GROUP
    is_e = (lane >= lo) & (lane < lo + EXPERTS_PER_GROUP)
    emax = lane_max(jnp.where(is_e, x, ninf))
    ee = jnp.where(is_e, jnp.exp(x - emax), 0.0)
    pe = ee / jnp.sum(ee, axis=-1, keepdims=True)
    pe = jnp.where(is_e, pe, ninf)
    p0 = lane_max(pe)
    l0 = first_lane(pe == p0)
    pe1 = jnp.where(lane == l0, ninf, pe)
    p1 = lane_max(pe1)
    l1 = first_lane(pe1 == p1)
    den = p0 + p1
    w0 = pg_top * p0 / den
    w1 = pg_top * p1 / den
    e0 = l0 - N_GROUPS
    e1 = l1 - N_GROUPS

    onehot = ((lane == e0) | (lane == e1)).astype(BF16)
    r = lax.broadcasted_iota(jnp.int32, (tr, tr), 0)
    c = lax.broadcasted_iota(jnp.int32, (tr, tr), 1)
    tri = jnp.where(c < r, 1.0, 0.0).astype(BF16)
    prefix = jnp.dot(tri, onehot, preferred_element_type=F32) + carry_scr[0:1, :]
    rank0 = jnp.sum(jnp.where(lane == e0, prefix, 0.0), axis=-1, keepdims=True)
    rank1 = jnp.sum(jnp.where(lane == e1, prefix, 0.0), axis=-1, keepdims=True)
    total = carry_scr[0:1, :] + jnp.sum(onehot.astype(F32), axis=0, keepdims=True)
    carry_scr[...] = jnp.broadcast_to(total, carry_scr.shape)
    cnt_ref[...] = jnp.broadcast_to(total, cnt_ref.shape)

    rec = jnp.zeros((tr, LANES), F32)
    for k, val in ((RT_E0, e0.astype(F32)), (RT_E1, e1.astype(F32)), (RT_W0, w0), (RT_W1, w1),
                   (RT_R0, rank0), (RT_R1, rank1)):
        rec = jnp.where(lane == k, val, rec)
    rt_ref[...] = rec


def _route(logits):
    t = logits.shape[0]
    tr = 512
    return pl.pallas_call(
        _route_kernel,
        grid=(t // tr,),
        in_specs=[pl.BlockSpec((tr, LANES), lambda i: (i, 0))],
        out_specs=[pl.BlockSpec((tr, LANES), lambda i: (i, 0)),
                   pl.BlockSpec((8, LANES), lambda i: (0, 0))],
        out_shape=[jax.ShapeDtypeStruct((t, LANES), F32),
                   jax.ShapeDtypeStruct((8, LANES), F32)],
        scratch_shapes=[pltpu.VMEM((8, LANES), F32)],
        compiler_params=_cparams(("arbitrary",)),
        name="route",
    )(logits)


def _row_copy(src_hbm, dst_buf, src_row, dst_row, sem):
    return pltpu.make_async_copy(src_hbm.at[pl.ds(src_row, 1), :], dst_buf.at[pl.ds(dst_row, 1), :], sem)


def _wait_rows(src_hbm, dst_buf, n_rows, sem):
    def body(r, _):
        _row_copy(src_hbm, dst_buf, 0, r, sem).wait()
        return 0

    lax.fori_loop(0, n_rows, body, 0, unroll=8)


def _moe_kernel(tok_ref, be_ref, nu_ref, run_ref, nxt_ref, h_hbm, w1_hbm, w3_hbm, w2_hbm, y_ref, *scratch):
    depth = GATHER_LOOKAHEAD
    bufs = scratch[:depth + 1]
    sem, w1_f, w3_f, w2_f, wsem, w1_bf, w3_bf, w2_bf = scratch[depth + 1:]
    rows = MOE_BLOCK
    i = pl.program_id(0)
    nused = nu_ref[0]

    def issue(block, parity):
        base = block * rows
        for r in range(rows):
            _row_copy(h_hbm, bufs[parity], tok_ref[base + r], r, sem.at[parity]).start()

    for first in range(depth):
        @pl.when((i == 0) & (first < nused))
        def _(first=first):
            issue(first, first)

    def weight_copies(expert, slot):
        return [pltpu.make_async_copy(src.at[expert], dst.at[slot], wsem.at[slot])
                for src, dst in ((w1_hbm, w1_f), (w3_hbm, w3_f), (w2_hbm, w2_f))]

    new_expert = (i == 0) | (be_ref[i] != be_ref[jnp.maximum(i - 1, 0)])
    wslot = run_ref[i] % 2

    @pl.when((i == 0) & (nused > 0))
    def _():
        for cp in weight_copies(be_ref[0], 0):
            cp.start()

    @pl.when((i < nused) & new_expert)
    def _():
        for cp in weight_copies(be_ref[i], wslot):
            cp.wait()

        @pl.when(nxt_ref[i] >= 0)
        def _():
            for cp in weight_copies(nxt_ref[i], 1 - wslot):
                cp.start()

        w1_bf[...] = w1_f[wslot].astype(BF16)
        w3_bf[...] = w3_f[wslot].astype(BF16)
        w2_bf[...] = w2_f[wslot].astype(BF16)

    def block_step(parity, prefetch):
        _wait_rows(h_hbm, bufs[parity], rows, sem.at[parity])
        if prefetch:
            issue(i + depth, (parity + depth) % len(bufs))
        xb = bufs[parity][...].astype(BF16)
        h1 = jnp.dot(xb, w1_bf[...], preferred_element_type=F32)
        h3 = jnp.dot(xb, w3_bf[...], preferred_element_type=F32)
        hid = (h1 * jax.nn.sigmoid(h1)) * h3
        y_ref[...] = jnp.dot(hid.astype(BF16), w2_bf[...], preferred_element_type=F32)

    for parity in range(len(bufs)):
        for prefetch in (True, False):
            more = (i + depth < nused) if prefetch else ((i < nused) & (i + depth >= nused))
            pl.when((i % len(bufs) == parity) & more)(functools.partial(block_step, parity, prefetch))

    @pl.when(i >= nused)
    def _():
        y_ref[...] = jnp.zeros_like(y_ref)


def _moe(row_tok, blk_e, nused, run_id, next_e, h2, w1, w3, w2):
    t, d = h2.shape
    r = row_tok.shape[0]
    nblk = r // MOE_BLOCK
    grid_spec = pltpu.PrefetchScalarGridSpec(
        num_scalar_prefetch=5,
        grid=(nblk,),
        in_specs=[pl.BlockSpec(memory_space=pl.ANY)] * 4,
        out_specs=pl.BlockSpec((MOE_BLOCK, d), lambda i, *_: (i, 0)),
        scratch_shapes=[pltpu.VMEM((MOE_BLOCK, d), F32)] * (GATHER_LOOKAHEAD + 1) + [
                        pltpu.SemaphoreType.DMA((GATHER_LOOKAHEAD + 1,)),
                        pltpu.VMEM((2, d, D_EXPERT), F32), pltpu.VMEM((2, d, D_EXPERT), F32),
                        pltpu.VMEM((2, D_EXPERT, d), F32), pltpu.SemaphoreType.DMA((2,)),
                        pltpu.VMEM((d, D_EXPERT), BF16), pltpu.VMEM((d, D_EXPERT), BF16),
                        pltpu.VMEM((D_EXPERT, d), BF16)],
    )
    return pl.pallas_call(
        _moe_kernel,
        grid_spec=grid_spec,
        out_shape=jax.ShapeDtypeStruct((r, d), F32),
        compiler_params=_cparams(("arbitrary",)),
        name="moe",
    )(row_tok, blk_e, nused, run_id, next_e, h2, w1, w3, w2)


def _final_kernel(dest_ref, y_hbm, x1_ref, rt_ref, gt_ref, g_ref, o_ref, *scratch, last_layer):
    depth = GATHER_LOOKAHEAD
    bufs, sem = scratch[:depth + 1], scratch[depth + 1]
    tf = x1_ref.shape[0]
    i = pl.program_id(0)
    n = pl.num_programs(0)

    def issue(step, parity):
        base = step * tf * TOP_K_INNER
        for r in range(tf):
            for k in range(TOP_K_INNER):
                _row_copy(y_hbm, bufs[parity].at[k], dest_ref[base + r * TOP_K_INNER + k], r,
                          sem.at[parity]).start()

    for first in range(depth):
        @pl.when((i == 0) & (first < n))
        def _(first=first):
            issue(first, first)

    def tile_step(parity, prefetch):
        buf = bufs[parity]
        for k in range(TOP_K_INNER):
            _wait_rows(y_hbm, buf.at[k], tf, sem.at[parity])
        if prefetch:
            issue(i + depth, (parity + depth) % len(bufs))
        rt = rt_ref[...]
        w0 = rt[:, RT_W0:RT_W0 + 1]
        w1 = rt[:, RT_W1:RT_W1 + 1]
        y = buf[0] * w0 + buf[1] * w1
        x2 = x1_ref[...] + gt_ref[0] * y
        if last_layer:
            ms = jnp.mean(x2 * x2, axis=-1, keepdims=True)
            x2 = x2 * lax.rsqrt(ms + EPS) * g_ref[...]
        o_ref[...] = x2

    for parity in range(len(bufs)):
        for prefetch in (True, False):
            more = (i + depth < n) if prefetch else (i + depth >= n)
            pl.when((i % len(bufs) == parity) & more)(functools.partial(tile_step, parity, prefetch))


def _final(dest, yr, x1, rt, gt2, final_g, seq, last_layer):
    t, d = x1.shape
    tf = 256
    bpt = seq // tf
    grid_spec = pltpu.PrefetchScalarGridSpec(
        num_scalar_prefetch=1,
        grid=(t // tf,),
        in_specs=[pl.BlockSpec(memory_space=pl.ANY),
                  pl.BlockSpec((tf, d), lambda i, dst: (i, 0)),
                  pl.BlockSpec((tf, LANES), lambda i, dst: (i, 0)),
                  pl.BlockSpec((1, 1, d), lambda i, dst: (i // bpt, 0, 0)),
                  pl.BlockSpec((1, d), lambda i, dst: (0, 0))],
        out_specs=pl.BlockSpec((tf, d), lambda i, dst: (i, 0)),
        scratch_shapes=[pltpu.VMEM((TOP_K_INNER, tf, d), F32)] * (GATHER_LOOKAHEAD + 1) + [
                        pltpu.SemaphoreType.DMA((GATHER_LOOKAHEAD + 1,))],
    )
    return pl.pallas_call(
        functools.partial(_final_kernel, last_layer=last_layer),
        grid_spec=grid_spec,
        out_shape=jax.ShapeDtypeStruct((t, d), F32),
        compiler_params=_cparams(("arbitrary",)),
        name="final",
    )(dest, yr, x1, rt, gt2, final_g)


def _dispatch_tables(rt, counts_row):
    t = rt.shape[0]
    tk = t * TOP_K_INNER
    eid = rt[:, RT_E0:RT_E1 + 1].astype(jnp.int32)
    rank = rt[:, RT_R0:RT_R1 + 1].astype(jnp.int32)
    counts = counts_row[:N_EXPERTS].astype(jnp.int32)
    padded = (counts + MOE_BLOCK - 1) // MOE_BLOCK * MOE_BLOCK
    pad_end = jnp.cumsum(padded)
    pad_start = pad_end - padded
    dest = (pad_start[eid] + rank).reshape(tk).astype(jnp.int32)
    r = tk + N_EXPERTS * MOE_BLOCK
    tok = jnp.repeat(jnp.arange(t, dtype=jnp.int32), TOP_K_INNER)
    row_tok = jnp.zeros((r,), jnp.int32).at[dest].set(tok)
    nblk = r // MOE_BLOCK
    blk_start = jnp.arange(nblk, dtype=jnp.int32) * MOE_BLOCK
    blk_e = jnp.minimum(jnp.sum(pad_end[None, :] <= blk_start[:, None], axis=1), N_EXPERTS - 1).astype(jnp.int32)
    nused = (pad_end[-1:] // MOE_BLOCK).astype(jnp.int32)
    idx = jnp.arange(nblk, dtype=jnp.int32)
    is_new = jnp.concatenate([jnp.ones((1,), bool), blk_e[1:] != blk_e[:-1]])
    run_id = (jnp.cumsum(is_new.astype(jnp.int32)) - 1).astype(jnp.int32)
    later = (idx[None, :] > idx[:, None]) & (blk_e[None, :] != blk_e[:, None]) & (idx[None, :] < nused[0])
    first_later = jnp.min(jnp.where(later, idx[None, :], nblk), axis=1)
    next_e = jnp.where(first_later < nblk, blk_e[jnp.minimum(first_later, nblk - 1)], -1).astype(jnp.int32)
    return dest, row_tok, blk_e, nused, run_id, next_e


def kernel(x, c, norm1_g, norm2_g, final_g, w_ada, b_ada, w_in, r_lower, r_norm_g,
           w_up_a, w_up_r, w_out, w_rg, b_rg, w_re, b_re, w1, w3, w2):
    batch, seq, d = x.shape
    t = batch * seq
    depth = w_in.shape[0]
    x2 = x.reshape(t, d)
    c_pad = jnp.zeros((-(-batch // 16) * 16, d), F32).at[:batch].set(c)
    for l in range(depth):
        mod = _ada(c_pad, w_ada[l], b_ada[l][None, :])[:batch]
        sh1, sc1, gt1, sh2, sc2, gt2 = [m[:, None, :] for m in jnp.split(mod, 6, axis=-1)]
        proj = _proj(x2, norm1_g[l][None, :], sc1, sh1, w_in[l], seq)
        att = _moba(proj, batch, seq)
        o_r = _hgrn(proj, r_lower, r_norm_g[l][None, :], batch, seq, l)
        w_router = jnp.zeros((d, LANES), F32).at[:, :N_GROUPS].set(w_rg[l]) \
            .at[:, N_GROUPS:N_GROUPS + N_EXPERTS].set(w_re[l])
        b_router = jnp.zeros((1, LANES), F32).at[0, :N_GROUPS].set(b_rg[l]) \
            .at[0, N_GROUPS:N_GROUPS + N_EXPERTS].set(b_re[l])
        w_router_top = _bf16_part(w_router)
        w_router = jnp.concatenate([w_router_top, w_router - w_router_top], axis=-1).astype(BF16)
        x1, h2, logits = _merge(att, o_r, proj, x2, gt1, w_up_a[l].astype(BF16), w_up_r[l].astype(BF16),
                                w_out[l].astype(BF16), norm2_g[l][None, :], sc2, sh2, w_router, b_router, seq)
        rt, cnt = _route(logits)
        dest, row_tok, blk_e, nused, run_id, next_e = _dispatch_tables(rt, cnt[0])
        yr = _moe(row_tok, blk_e, nused, run_id, next_e, h2, w1[l], w3[l], w2[l])
        x2 = _final(dest, yr, x1, rt, gt2, final_g[None, :], seq, l == depth - 1)
    return x2.reshape(batch, seq, d)
```

```python
import functools

import jax
import jax.numpy as jnp
from jax import lax
from jax.experimental import pallas as pl
from jax.experimental.pallas import tpu as pltpu

F32 = jnp.float32
BF16 = jnp.bfloat16
HIGHEST = lax.Precision.HIGHEST

D_MODEL = 2048
A_HEADS = 8
A_HEAD_DIM = 128
A_WIDTH = A_HEADS * A_HEAD_DIM
MOBA_BLOCK = 256
MOBA_TOPK = 3
R_HEADS = 8
R_KEY_DIM = 128
R_VAL_DIM = 128
R_WIDTH = R_HEADS * R_KEY_DIM
N_GROUPS = 4
EXPERTS_PER_GROUP = 8
N_EXPERTS = N_GROUPS * EXPERTS_PER_GROUP
TOP_K_INNER = 2
D_EXPERT = 512
MOE_BLOCK = 256
EPS = 1e-6
IN_COLS = 3 * A_WIDTH + 4 * R_WIDTH + 2 * D_MODEL

LANES = 128
SUBLANES = 8
CB_AQ = 0
CB_AK = CB_AQ + A_WIDTH // LANES
CB_AV = CB_AK + A_WIDTH // LANES
CB_RQ = CB_AV + A_WIDTH // LANES
CB_RF = CB_RQ + R_WIDTH // LANES
CB_RI = CB_RF + R_WIDTH // LANES
CB_ROG = CB_RI + R_WIDTH // LANES
CB_GA = CB_ROG + R_WIDTH // LANES
CB_GR = CB_GA + D_MODEL // LANES

MOBA_GROUP = 4
Q_PRESCALE = A_HEAD_DIM ** -0.5 * 1.4426950408889634
HGRN_TILE = 512
HGRN_HEADS_PER_STEP = 4
HGRN_CHUNK = 64
HGRN_SUB = 8
GATHER_LOOKAHEAD = 4
NEG_BIG = -1e30
VMEM_LIMIT = 56 * 1024 * 1024


def _cparams(sem):
    return pltpu.CompilerParams(dimension_semantics=sem, vmem_limit_bytes=VMEM_LIMIT)


def _dot_nt(a, b, **kw):
    return lax.dot_general(a, b, (((1,), (1,)), ((), ())), preferred_element_type=F32, **kw)


def _bf16_part(a):
    bits = lax.bitcast_convert_type(a, jnp.uint32) & jnp.uint32(0xFFFF0000)
    return lax.bitcast_convert_type(bits, F32)


def _dot_tn(a, b):
    return lax.dot_general(a, b, (((0,), (0,)), ((), ())), preferred_element_type=F32)


def _ada_kernel(c_ref, w_ref, b_ref, o_ref):
    c = c_ref[...]
    ca = c * jax.nn.sigmoid(c)
    w = w_ref[...]
    ca_top, w_top = _bf16_part(ca), _bf16_part(w)
    ca_hi, ca_lo = ca_top.astype(BF16), (ca - ca_top).astype(BF16)
    w_hi, w_lo = w_top.astype(BF16), (w - w_top).astype(BF16)
    acc = jnp.dot(ca_hi, w_hi, preferred_element_type=F32) + jnp.dot(ca_lo, w_hi, preferred_element_type=F32)
    o_ref[...] = acc + jnp.dot(ca_hi, w_lo, preferred_element_type=F32) + b_ref[...]


def _ada(c_pad, w_ada, b_ada):
    rows, d = c_pad.shape
    n = w_ada.shape[1]
    tn = 1024
    return pl.pallas_call(
        _ada_kernel,
        grid=(n // tn,),
        in_specs=[pl.BlockSpec((rows, d), lambda j: (0, 0)),
                  pl.BlockSpec((d, tn), lambda j: (0, j)),
                  pl.BlockSpec((1, tn), lambda j: (0, j))],
        out_specs=pl.BlockSpec((rows, tn), lambda j: (0, j)),
        out_shape=jax.ShapeDtypeStruct((rows, n), F32),
        compiler_params=_cparams(("arbitrary",)),
        name="ada",
    )(c_pad, w_ada, b_ada)


def _proj_kernel(x_ref, g_ref, sc_ref, sh_ref, w_ref, cs_ref, o_ref, h_scr):
    @pl.when(pl.program_id(1) == 0)
    def _():
        x = x_ref[...]
        ms = jnp.mean(x * x, axis=-1, keepdims=True)
        y = x * lax.rsqrt(ms + EPS) * g_ref[...]
        h_scr[...] = (y * (1.0 + sc_ref[0]) + sh_ref[0]).astype(BF16)

    acc = jnp.dot(h_scr[...], w_ref[...].astype(BF16), preferred_element_type=F32)
    o_ref[...] = (acc * cs_ref[...]).astype(o_ref.dtype)


def _proj(x2, g, sc, sh, w_in, seq):
    t, d = x2.shape
    n = w_in.shape[1]
    tm = min(1024, seq)
    tn = 1024
    bpt = seq // tm
    col_scale = jnp.ones((1, n), F32).at[:, CB_AQ * LANES:CB_AK * LANES].set(Q_PRESCALE)
    return pl.pallas_call(
        _proj_kernel,
        grid=(t // tm, n // tn),
        in_specs=[pl.BlockSpec((tm, d), lambda i, j: (i, 0)),
                  pl.BlockSpec((1, d), lambda i, j: (0, 0)),
                  pl.BlockSpec((1, 1, d), lambda i, j: (i // bpt, 0, 0)),
                  pl.BlockSpec((1, 1, d), lambda i, j: (i // bpt, 0, 0)),
                  pl.BlockSpec((d, tn), lambda i, j: (0, j)),
                  pl.BlockSpec((1, tn), lambda i, j: (0, j))],
        out_specs=pl.BlockSpec((tm, tn), lambda i, j: (i, j)),
        out_shape=jax.ShapeDtypeStruct((t, n), BF16),
        scratch_shapes=[pltpu.VMEM((tm, d), BF16)],
        compiler_params=_cparams(("arbitrary", "arbitrary")),
        name="proj",
    )(x2, g, sc, sh, w_in, col_scale)


def _moba_kernel(q_ref, k_ref, v_ref, o_ref, kaug_scr, vt_scr, kmean_scr, s_scr, *, nb, gr):
    blk = MOBA_BLOCK
    gk = MOBA_GROUP * blk
    seq = k_ref.shape[0]

    k = k_ref[...]
    kaug_scr[:, :LANES] = k
    row_blk = lax.broadcasted_iota(jnp.int32, (seq, LANES), 0) // blk
    lane = lax.broadcasted_iota(jnp.int32, (seq, LANES), 1)
    kaug_scr[:, LANES:] = jnp.where(lane == row_blk, 1.0, 0.0).astype(BF16)
    kmean_scr[...] = jnp.zeros_like(kmean_scr)
    kmean_scr[0:nb, :] = jnp.mean(k.astype(F32).reshape(nb, blk, LANES), axis=1)
    for j in range(nb):
        vj = v_ref[j * blk:(j + 1) * blk, :].astype(F32)
        vt_scr[0:LANES, j * blk:(j + 1) * blk] = vj.T.astype(BF16)
    vt_scr[LANES:LANES + 16, :] = jnp.where(
        lax.broadcasted_iota(jnp.int32, (16, seq), 0) == 0, 1.0, 0.0).astype(BF16)

    rid = lax.broadcasted_iota(jnp.int32, (gr, blk), 0)

    def scores(i):
        q = q_ref[i * blk:(i + 1) * blk, :]
        gate = _dot_nt(kmean_scr[...], q.astype(F32), precision=HIGHEST)
        gate = jnp.where(rid < i, gate, -jnp.inf)
        sel = rid == i
        for _ in range(MOBA_TOPK):
            m = jnp.max(gate, axis=0, keepdims=True)
            idx = jnp.min(jnp.where(gate == m, rid, gr), axis=0, keepdims=True)
            pick = rid == idx
            sel = sel | (pick & (rid < i))
            gate = jnp.where(pick, -jnp.inf, gate)
        bias_t = jnp.where(sel, 0.0, NEG_BIG)
        bias_t = jnp.concatenate([bias_t, jnp.zeros((LANES - gr, blk), F32)], axis=0)
        q_aug = jnp.concatenate([q, bias_t.T.astype(BF16)], axis=-1)
        n = (i + 1) * blk
        spans = [(r0, min(r0 + gk, n)) for r0 in range(0, n, gk)]
        slot = i % 2
        mx = jnp.full((8, blk), NEG_BIG, F32)
        for r0, r1 in spans:
            s = _dot_nt(kaug_scr[r0:r1, :], q_aug)
            if r1 == n:
                kpos = r0 + lax.broadcasted_iota(jnp.int32, (r1 - r0, blk), 0)
                qpos = i * blk + lax.broadcasted_iota(jnp.int32, (r1 - r0, blk), 1)
                s = jnp.where(kpos <= qpos, s, NEG_BIG)
            s_scr[slot, r0:r1, :] = s
            mx = jnp.maximum(mx, jnp.max(s.reshape((r1 - r0) // 8, 8, blk), axis=0))
        return jnp.max(mx, axis=0, keepdims=True), spans

    def weighted_values(i, m, spans):
        slot = i % 2
        acc = jnp.zeros((LANES + 16, blk), F32)
        for r0, r1 in spans:
            p = jnp.exp2(s_scr[slot, r0:r1, :] - m).astype(BF16)
            acc = acc + jnp.dot(vt_scr[:, r0:r1], p, preferred_element_type=F32)
        out_t = acc[0:LANES] / acc[LANES:LANES + 1]
        o_ref[i * blk:(i + 1) * blk, :] = out_t.T.astype(o_ref.dtype)

    pending = scores(0)
    for i in range(nb):
        nxt = scores(i + 1) if i + 1 < nb else None
        weighted_values(i, *pending)
        pending = nxt


def _moba(proj, batch, seq):
    nb = seq // MOBA_BLOCK
    blk = MOBA_BLOCK
    gr = -(-nb // 8) * 8
    t = batch * seq
    assert gr <= LANES
    return pl.pallas_call(
        functools.partial(_moba_kernel, nb=nb, gr=gr),
        grid=(batch, A_HEADS),
        in_specs=[pl.BlockSpec((seq, LANES), lambda b, h: (b, CB_AQ + h)),
                  pl.BlockSpec((seq, LANES), lambda b, h: (b, CB_AK + h)),
                  pl.BlockSpec((seq, LANES), lambda b, h: (b, CB_AV + h))],
        out_specs=pl.BlockSpec((seq, LANES), lambda b, h: (b, h)),
        out_shape=jax.ShapeDtypeStruct((t, A_WIDTH), BF16),
        scratch_shapes=[pltpu.VMEM((seq, 2 * LANES), BF16),
                        pltpu.VMEM((LANES + 16, seq), BF16),
                        pltpu.VMEM((gr, LANES), F32),
                        pltpu.VMEM((2, seq, blk), F32)],
        compiler_params=_cparams(("arbitrary", "arbitrary")),
        name="moba",
    )(proj, proj, proj)


def _hgrn_kernel(q_ref, f_ref, i_ref, og_ref, rl_ref, g_ref, o_ref, st_scr, b_scr, *, layer):
    @pl.when(pl.program_id(2) == 0)
    def _():
        st_scr[...] = jnp.zeros_like(st_scr)

    for hh in range(q_ref.shape[1] // LANES):
        cs = slice(hh * LANES, (hh + 1) * LANES)
        _hgrn_head(q_ref.at[:, cs], f_ref.at[:, cs], i_ref.at[:, cs], og_ref.at[:, cs], rl_ref.at[:, cs],
                   g_ref, o_ref.at[:, cs], st_scr.at[hh], b_scr.at[hh], layer=layer)


def _hgrn_head(q_ref, f_ref, i_ref, og_ref, rl_ref, g_ref, o_ref, st_scr, b_scr, *, layer):
    tt = q_ref.shape[0]
    ch, sub = HGRN_CHUNK, HGRN_SUB
    ns, nc = ch // sub, tt // ch
    assert SUBLANES % sub == 0 and ch % SUBLANES == 0 and tt % ch == 0

    rl = rl_ref[...]
    e = jnp.exp(rl - jnp.max(rl, axis=0, keepdims=True))
    lb = jnp.sum(e[: layer + 1], axis=0, keepdims=True) / jnp.sum(e, axis=0, keepdims=True)

    q = q_ref[...].astype(F32)
    x = f_ref[...].astype(F32)
    v_bf = i_ref[...]
    v = v_bf.astype(F32)
    f = lb + (1.0 - lb) * jax.nn.sigmoid(x)
    kin = (1.0 - lb) * jax.nn.sigmoid(-x)
    row = lax.broadcasted_iota(jnp.int32, (tt, LANES), 0)
    pic = row % ch
    pos = row % sub

    def roll8(a, k):
        return pltpu.roll(a.reshape(tt // SUBLANES, SUBLANES, LANES), k, 1).reshape(tt, LANES)

    b = jnp.log2(f)
    pos8 = row % SUBLANES
    step = 1
    while step < SUBLANES:
        b = b + jnp.where(pos8 >= step, roll8(b, step), 0.0)
        step *= 2
    gpc = ch // SUBLANES
    b4 = b.reshape(nc, gpc, SUBLANES, LANES)
    offs = [jnp.zeros((nc, 1, 1, LANES), F32)]
    for g in range(1, gpc):
        offs.append(offs[-1] + b4[:, g - 1:g, SUBLANES - 1:SUBLANES, :])
    b = (b4 + jnp.concatenate(offs, axis=1)).reshape(tt, LANES)
    b_scr[...] = b

    r_sub = b
    for k in range(1, sub):
        r_sub = jnp.where(pos == sub - 1 - k, roll8(b, SUBLANES - k), r_sub)
    kh = kin * jnp.exp2(r_sub - b)

    diag = jnp.sum(q * kin, axis=-1, keepdims=True) * v
    for delta in range(1, sub):
        kr = roll8(kin, delta)
        br = roll8(b, delta)
        vr = roll8(v, delta)
        w = jnp.where(pos >= delta, q * kr * jnp.exp2(b - br), 0.0)
        diag = diag + jnp.sum(w, axis=-1, keepdims=True) * vr

    pc = lax.broadcasted_iota(jnp.int32, (ch, LANES), 0)
    sid = pc // sub
    st = st_scr[...]
    outs = []
    for c in range(nc):
        sl = slice(c * ch, (c + 1) * ch)
        q_c, b_c, kh_c, vc = q[sl], b[sl], kh[sl], v_bf[sl]
        q_blocks, k_blocks = [], []
        for j in range(ns - 1):
            rj = b_scr[pl.ds(c * ch + sub * (j + 1) - 1, 1), :]
            qj = q_c * jnp.exp2(jnp.where(pc >= sub * (j + 1), b_c - rj, -jnp.inf))
            q_blocks.append(qj.astype(BF16))
            k_blocks.append(jnp.where(sid == j, kh_c, 0.0).astype(BF16))
        a_off = _dot_nt(jnp.concatenate(q_blocks, axis=-1), jnp.concatenate(k_blocks, axis=-1))
        o_c = jnp.dot(a_off.astype(BF16), vc, preferred_element_type=F32)
        o_c = o_c + _dot_nt((q_c * jnp.exp2(b_c)).astype(BF16), st.astype(BF16)) + diag[sl]
        r_last = b_scr[pl.ds((c + 1) * ch - 1, 1), :]
        ke = (kh_c * jnp.exp2(r_last - r_sub[sl])).astype(BF16)
        st = jnp.exp2(r_last) * st + _dot_tn(vc, ke)
        outs.append(o_c)
    st_scr[...] = st
    o = jnp.concatenate(outs, axis=0)

    y = o * lax.rsqrt(jnp.mean(o * o, axis=-1, keepdims=True) + EPS) * g_ref[...]
    og = og_ref[...].astype(F32)
    o_ref[...] = (y * (og * jax.nn.sigmoid(og))).astype(o_ref.dtype)


def _hgrn(proj, r_lower, r_norm_g, batch, seq, layer):
    tt = min(HGRN_TILE, seq)
    nt = seq // tt
    t = batch * seq
    nl = r_lower.shape[0]

    nh = HGRN_HEADS_PER_STEP
    hw = nh * LANES
    assert R_HEADS % nh == 0 and all(cb % nh == 0 for cb in (CB_RQ, CB_RF, CB_RI, CB_ROG))

    def col(cb):
        return pl.BlockSpec((tt, hw), lambda b, h, c: (b * nt + c, cb // nh + h))

    return pl.pallas_call(
        functools.partial(_hgrn_kernel, layer=layer),
        grid=(batch, R_HEADS // nh, nt),
        in_specs=[col(CB_RQ), col(CB_RF), col(CB_RI), col(CB_ROG),
                  pl.BlockSpec((nl, hw), lambda b, h, c: (0, h)),
                  pl.BlockSpec((1, LANES), lambda b, h, c: (0, 0))],
        out_specs=pl.BlockSpec((tt, hw), lambda b, h, c: (b * nt + c, h)),
        out_shape=jax.ShapeDtypeStruct((t, R_WIDTH), BF16),
        scratch_shapes=[pltpu.VMEM((nh, LANES, LANES), F32), pltpu.VMEM((nh, tt, LANES), F32)],
        compiler_params=_cparams(("arbitrary", "arbitrary", "arbitrary")),
        name="hgrn",
    )(proj, proj, proj, proj, r_lower, r_norm_g)


def _merge_kernel(att_ref, or_ref, ga0_ref, ga1_ref, gr0_ref, gr1_ref, x_ref, gt_ref,
                  wua_ref, wur_ref, wo_ref, g2_ref, sc_ref, sh_ref, wr_ref, br_ref,
                  x1_ref, h2_ref, lg_ref):
    ya = jnp.dot(att_ref[...], wua_ref[...], preferred_element_type=F32)
    yr = jnp.dot(or_ref[...], wur_ref[...], preferred_element_type=F32)
    ga = jnp.concatenate([ga0_ref[...], ga1_ref[...]], axis=-1).astype(F32)
    gr = jnp.concatenate([gr0_ref[...], gr1_ref[...]], axis=-1).astype(F32)
    merged = jax.nn.sigmoid(ga) * ya + jax.nn.sigmoid(gr) * yr
    out = jnp.dot(merged.astype(BF16), wo_ref[...], preferred_element_type=F32)
    x1 = x_ref[...] + gt_ref[0] * out
    x1_ref[...] = x1
    ms = jnp.mean(x1 * x1, axis=-1, keepdims=True)
    h2 = (x1 * lax.rsqrt(ms + EPS) * g2_ref[...]) * (1.0 + sc_ref[0]) + sh_ref[0]
    h2_ref[...] = h2
    h2_top = _bf16_part(h2)
    h2_hi = h2_top.astype(BF16)
    h2_lo = (h2 - h2_top).astype(BF16)
    wr = wr_ref[...]
    p_hi = jnp.dot(h2_hi, wr, preferred_element_type=F32)
    p_lo = jnp.dot(h2_lo, wr[:, :LANES], preferred_element_type=F32)
    lg_ref[...] = p_hi[:, :LANES] + p_hi[:, LANES:] + p_lo + br_ref[...]


def _merge(att, o_r, proj, x2, gt1, w_up_a, w_up_r, w_out, g2, sc2, sh2, w_router, b_router, seq):
    t, d = x2.shape
    tm = 256
    bpt = seq // tm
    half = d // 2
    cb = half // LANES

    def gspec(cb0, k):
        return pl.BlockSpec((tm, half), lambda i: (i, cb0 // cb + k))

    def const(shape):
        return pl.BlockSpec(shape, lambda i: tuple(0 for _ in shape), pipeline_mode=pl.Buffered(1))

    def per_batch():
        return pl.BlockSpec((1, 1, d), lambda i: (i // bpt, 0, 0))

    return pl.pallas_call(
        _merge_kernel,
        grid=(t // tm,),
        in_specs=[pl.BlockSpec((tm, A_WIDTH), lambda i: (i, 0)),
                  pl.BlockSpec((tm, R_WIDTH), lambda i: (i, 0)),
                  gspec(CB_GA, 0), gspec(CB_GA, 1), gspec(CB_GR, 0), gspec(CB_GR, 1),
                  pl.BlockSpec((tm, d), lambda i: (i, 0)),
                  per_batch(),
                  const((A_WIDTH, d)), const((R_WIDTH, d)), const((d, d)),
                  const((1, d)), per_batch(), per_batch(),
                  const((d, 2 * LANES)), const((1, LANES))],
        out_specs=[pl.BlockSpec((tm, d), lambda i: (i, 0)),
                   pl.BlockSpec((tm, d), lambda i: (i, 0)),
                   pl.BlockSpec((tm, LANES), lambda i: (i, 0))],
        out_shape=[jax.ShapeDtypeStruct((t, d), F32),
                   jax.ShapeDtypeStruct((t, d), F32),
                   jax.ShapeDtypeStruct((t, LANES), F32)],
        compiler_params=_cparams(("arbitrary",)),
        name="merge",
    )(att, o_r, proj, proj, proj, proj, x2, gt1, w_up_a, w_up_r, w_out, g2, sc2, sh2, w_router, b_router)


RT_E0, RT_E1, RT_W0, RT_W1, RT_R0, RT_R1 = 0, 1, 2, 3, 4, 5


def _route_kernel(lg_ref, rt_ref, cnt_ref, carry_scr):
    tr = lg_ref.shape[0]

    @pl.when(pl.program_id(0) == 0)
    def _():
        carry_scr[...] = jnp.zeros_like(carry_scr)

    x = lg_ref[...]
    lane = lax.broadcasted_iota(jnp.int32, (tr, LANES), 1)
    ninf = -jnp.inf

    def lane_max(val):
        return jnp.max(val, axis=-1, keepdims=True)

    def first_lane(mask):
        return jnp.min(jnp.where(mask, lane, LANES), axis=-1, keepdims=True)

    is_g = lane < N_GROUPS
    gmax = lane_max(jnp.where(is_g, x, ninf))
    grp = first_lane(is_g & (x == gmax))
    eg = jnp.where(is_g, jnp.exp(x - gmax), 0.0)
    pg_top = 1.0 / jnp.sum(eg, axis=-1, keepdims=True)

    lo = N_GROUPS + grp * EXPERTS_PER_GROUP
    is_e = (lane >= lo) & (lane < lo + EXPERTS_PER_GROUP)
    emax = lane_max(jnp.where(is_e, x, ninf))
    ee = jnp.where(is_e, jnp.exp(x - emax), 0.0)
    pe = ee / jnp.sum(ee, axis=-1, keepdims=True)
    pe = jnp.where(is_e, pe, ninf)
    p0 = lane_max(pe)
    l0 = first_lane(pe == p0)
    pe1 = jnp.where(lane == l0, ninf, pe)
    p1 = lane_max(pe1)
    l1 = first_lane(pe1 == p1)
    den = p0 + p1
    w0 = pg_top * p0 / den
    w1 = pg_top * p1 / den
    e0 = l0 - N_GROUPS
    e1 = l1 - N_GROUPS

    onehot = ((lane == e0) | (lane == e1)).astype(BF16)
    r = lax.broadcasted_iota(jnp.int32, (tr, tr), 0)
    c = lax.broadcasted_iota(jnp.int32, (tr, tr), 1)
    tri = jnp.where(c < r, 1.0, 0.0).astype(BF16)
    prefix = jnp.dot(tri, onehot, preferred_element_type=F32) + carry_scr[0:1, :]
    rank0 = jnp.sum(jnp.where(lane == e0, prefix, 0.0), axis=-1, keepdims=True)
    rank1 = jnp.sum(jnp.where(lane == e1, prefix, 0.0), axis=-1, keepdims=True)
    total = carry_scr[0:1, :] + jnp.sum(onehot.astype(F32), axis=0, keepdims=True)
    carry_scr[...] = jnp.broadcast_to(total, carry_scr.shape)
    cnt_ref[...] = jnp.broadcast_to(total, cnt_ref.shape)

    rec = jnp.zeros((tr, LANES), F32)
    for k, val in ((RT_E0, e0.astype(F32)), (RT_E1, e1.astype(F32)), (RT_W0, w0), (RT_W1, w1),
                   (RT_R0, rank0), (RT_R1, rank1)):
        rec = jnp.where(lane == k, val, rec)
    rt_ref[...] = rec


def _route(logits):
    t = logits.shape[0]
    tr = 512
    return pl.pallas_call(
        _route_kernel,
        grid=(t // tr,),
        in_specs=[pl.BlockSpec((tr, LANES), lambda i: (i, 0))],
        out_specs=[pl.BlockSpec((tr, LANES), lambda i: (i, 0)),
                   pl.BlockSpec((8, LANES), lambda i: (0, 0))],
        out_shape=[jax.ShapeDtypeStruct((t, LANES), F32),
                   jax.ShapeDtypeStruct((8, LANES), F32)],
        scratch_shapes=[pltpu.VMEM((8, LANES), F32)],
        compiler_params=_cparams(("arbitrary",)),
        name="route",
    )(logits)


def _row_copy(src_hbm, dst_buf, src_row, dst_row, sem):
    return pltpu.make_async_copy(src_hbm.at[pl.ds(src_row, 1), :], dst_buf.at[pl.ds(dst_row, 1), :], sem)


def _wait_rows(src_hbm, dst_buf, n_rows, sem):
    def body(r, _):
        _row_copy(src_hbm, dst_buf, 0, r, sem).wait()
        return 0

    lax.fori_loop(0, n_rows, body, 0, unroll=8)


def _moe_kernel(tok_ref, be_ref, nu_ref, run_ref, nxt_ref, h_hbm, w1_hbm, w3_hbm, w2_hbm, y_ref, *scratch):
    depth = GATHER_LOOKAHEAD
    bufs = scratch[:depth + 1]
    sem, w1_f, w3_f, w2_f, wsem, w1_bf, w3_bf, w2_bf = scratch[depth + 1:]
    rows = MOE_BLOCK
    i = pl.program_id(0)
    nused = nu_ref[0]

    def issue(block, parity):
        base = block * rows
        for r in range(rows):
            _row_copy(h_hbm, bufs[parity], tok_ref[base + r], r, sem.at[parity]).start()

    for first in range(depth):
        @pl.when((i == 0) & (first < nused))
        def _(first=first):
            issue(first, first)

    def weight_copies(expert, slot):
        return [pltpu.make_async_copy(src.at[expert], dst.at[slot], wsem.at[slot])
                for src, dst in ((w1_hbm, w1_f), (w3_hbm, w3_f), (w2_hbm, w2_f))]

    new_expert = (i == 0) | (be_ref[i] != be_ref[jnp.maximum(i - 1, 0)])
    wslot = run_ref[i] % 2

    @pl.when((i == 0) & (nused > 0))
    def _():
        for cp in weight_copies(be_ref[0], 0):
            cp.start()

    @pl.when((i < nused) & new_expert)
    def _():
        for cp in weight_copies(be_ref[i], wslot):
            cp.wait()

        @pl.when(nxt_ref[i] >= 0)
        def _():
            for cp in weight_copies(nxt_ref[i], 1 - wslot):
                cp.start()

        w1_bf[...] = w1_f[wslot].astype(BF16)
        w3_bf[...] = w3_f[wslot].astype(BF16)
        w2_bf[...] = w2_f[wslot].astype(BF16)

    def block_step(parity, prefetch):
        _wait_rows(h_hbm, bufs[parity], rows, sem.at[parity])
        if prefetch:
            issue(i + depth, (parity + depth) % len(bufs))
        xb = bufs[parity][...].astype(BF16)
        h1 = jnp.dot(xb, w1_bf[...], preferred_element_type=F32)
        h3 = jnp.dot(xb, w3_bf[...], preferred_element_type=F32)
        hid = (h1 * jax.nn.sigmoid(h1)) * h3
        y_ref[...] = jnp.dot(hid.astype(BF16), w2_bf[...], preferred_element_type=F32)

    for parity in range(len(bufs)):
        for prefetch in (True, False):
            more = (i + depth < nused) if prefetch else ((i < nused) & (i + depth >= nused))
            pl.when((i % len(bufs) == parity) & more)(functools.partial(block_step, parity, prefetch))

    @pl.when(i >= nused)
    def _():
        y_ref[...] = jnp.zeros_like(y_ref)


def _moe(row_tok, blk_e, nused, run_id, next_e, h2, w1, w3, w2):
    t, d = h2.shape
    r = row_tok.shape[0]
    nblk = r // MOE_BLOCK
    grid_spec = pltpu.PrefetchScalarGridSpec(
        num_scalar_prefetch=5,
        grid=(nblk,),
        in_specs=[pl.BlockSpec(memory_space=pl.ANY)] * 4,
        out_specs=pl.BlockSpec((MOE_BLOCK, d), lambda i, *_: (i, 0)),
        scratch_shapes=[pltpu.VMEM((MOE_BLOCK, d), F32)] * (GATHER_LOOKAHEAD + 1) + [
                        pltpu.SemaphoreType.DMA((GATHER_LOOKAHEAD + 1,)),
                        pltpu.VMEM((2, d, D_EXPERT), F32), pltpu.VMEM((2, d, D_EXPERT), F32),
                        pltpu.VMEM((2, D_EXPERT, d), F32), pltpu.SemaphoreType.DMA((2,)),
                        pltpu.VMEM((d, D_EXPERT), BF16), pltpu.VMEM((d, D_EXPERT), BF16),
                        pltpu.VMEM((D_EXPERT, d), BF16)],
    )
    return pl.pallas_call(
        _moe_kernel,
        grid_spec=grid_spec,
        out_shape=jax.ShapeDtypeStruct((r, d), F32),
        compiler_params=_cparams(("arbitrary",)),
        name="moe",
    )(row_tok, blk_e, nused, run_id, next_e, h2, w1, w3, w2)


def _final_kernel(dest_ref, y_hbm, x1_ref, rt_ref, gt_ref, g_ref, o_ref, *scratch, last_layer):
    depth = GATHER_LOOKAHEAD
    bufs, sem = scratch[:depth + 1], scratch[depth + 1]
    tf = x1_ref.shape[0]
    i = pl.program_id(0)
    n = pl.num_programs(0)

    def issue(step, parity):
        base = step * tf * TOP_K_INNER
        for r in range(tf):
            for k in range(TOP_K_INNER):
                _row_copy(y_hbm, bufs[parity].at[k], dest_ref[base + r * TOP_K_INNER + k], r,
                          sem.at[parity]).start()

    for first in range(depth):
        @pl.when((i == 0) & (first < n))
        def _(first=first):
            issue(first, first)

    def tile_step(parity, prefetch):
        buf = bufs[parity]
        for k in range(TOP_K_INNER):
            _wait_rows(y_hbm, buf.at[k], tf, sem.at[parity])
        if prefetch:
            issue(i + depth, (parity + depth) % len(bufs))
        rt = rt_ref[...]
        w0 = rt[:, RT_W0:RT_W0 + 1]
        w1 = rt[:, RT_W1:RT_W1 + 1]
        y = buf[0] * w0 + buf[1] * w1
        x2 = x1_ref[...] + gt_ref[0] * y
        if last_layer:
            ms = jnp.mean(x2 * x2, axis=-1, keepdims=True)
            x2 = x2 * lax.rsqrt(ms + EPS) * g_ref[...]
        o_ref[...] = x2

    for parity in range(len(bufs)):
        for prefetch in (True, False):
            more = (i + depth < n) if prefetch else (i + depth >= n)
            pl.when((i % len(bufs) == parity) & more)(functools.partial(tile_step, parity, prefetch))


def _final(dest, yr, x1, rt, gt2, final_g, seq, last_layer):
    t, d = x1.shape
    tf = 256
    bpt = seq // tf
    grid_spec = pltpu.PrefetchScalarGridSpec(
        num_scalar_prefetch=1,
        grid=(t // tf,),
        in_specs=[pl.BlockSpec(memory_space=pl.ANY),
                  pl.BlockSpec((tf, d), lambda i, dst: (i, 0)),
                  pl.BlockSpec((tf, LANES), lambda i, dst: (i, 0)),
                  pl.BlockSpec((1, 1, d), lambda i, dst: (i // bpt, 0, 0)),
                  pl.BlockSpec((1, d), lambda i, dst: (0, 0))],
        out_specs=pl.BlockSpec((tf, d), lambda i, dst: (i, 0)),
        scratch_shapes=[pltpu.VMEM((TOP_K_INNER, tf, d), F32)] * (GATHER_LOOKAHEAD + 1) + [
                        pltpu.SemaphoreType.DMA((GATHER_LOOKAHEAD + 1,))],
    )
    return pl.pallas_call(
        functools.partial(_final_kernel, last_layer=last_layer),
        grid_spec=grid_spec,
        out_shape=jax.ShapeDtypeStruct((t, d), F32),
        compiler_params=_cparams(("arbitrary",)),
        name="final",
    )(dest, yr, x1, rt, gt2, final_g)


def _dispatch_tables(rt, counts_row):
    t = rt.shape[0]
    tk = t * TOP_K_INNER
    eid = rt[:, RT_E0:RT_E1 + 1].astype(jnp.int32)
    rank = rt[:, RT_R0:RT_R1 + 1].astype(jnp.int32)
    counts = counts_row[:N_EXPERTS].astype(jnp.int32)
    padded = (counts + MOE_BLOCK - 1) // MOE_BLOCK * MOE_BLOCK
    pad_end = jnp.cumsum(padded)
    pad_start = pad_end - padded
    onehot = (eid[..., None] == jnp.arange(N_EXPERTS, dtype=jnp.int32)).astype(F32)
    base = jnp.einsum('tke,e->tk', onehot, pad_start.astype(F32), precision=HIGHEST)
    dest = (base.astype(jnp.int32) + rank).reshape(tk)
    r = tk + N_EXPERTS * MOE_BLOCK
    tok = jnp.repeat(jnp.arange(t, dtype=jnp.int32), TOP_K_INNER)
    row_tok = jnp.zeros((r,), jnp.int32).at[dest].set(tok)
    nblk = r // MOE_BLOCK
    blk_start = jnp.arange(nblk, dtype=jnp.int32) * MOE_BLOCK
    blk_e = jnp.minimum(jnp.sum(pad_end[None, :] <= blk_start[:, None], axis=1), N_EXPERTS - 1).astype(jnp.int32)
    nused = (pad_end[-1:] // MOE_BLOCK).astype(jnp.int32)
    idx = jnp.arange(nblk, dtype=jnp.int32)
    is_new = jnp.concatenate([jnp.ones((1,), bool), blk_e[1:] != blk_e[:-1]])
    run_id = (jnp.cumsum(is_new.astype(jnp.int32)) - 1).astype(jnp.int32)
    later = (idx[None, :] > idx[:, None]) & (blk_e[None, :] != blk_e[:, None]) & (idx[None, :] < nused[0])
    first_later = jnp.min(jnp.where(later, idx[None, :], nblk), axis=1)
    next_e = jnp.where(first_later < nblk, blk_e[jnp.minimum(first_later, nblk - 1)], -1).astype(jnp.int32)
    return dest, row_tok, blk_e, nused, run_id, next_e


def kernel(x, c, norm1_g, norm2_g, final_g, w_ada, b_ada, w_in, r_lower, r_norm_g,
           w_up_a, w_up_r, w_out, w_rg, b_rg, w_re, b_re, w1, w3, w2):
    batch, seq, d = x.shape
    t = batch * seq
    depth = w_in.shape[0]
    x2 = x.reshape(t, d)
    c_pad = jnp.zeros((-(-batch // 16) * 16, d), F32).at[:batch].set(c)
    for l in range(depth):
        mod = _ada(c_pad, w_ada[l], b_ada[l][None, :])[:batch]
        sh1, sc1, gt1, sh2, sc2, gt2 = [m[:, None, :] for m in jnp.split(mod, 6, axis=-1)]
        proj = _proj(x2, norm1_g[l][None, :], sc1, sh1, w_in[l], seq)
        att = _moba(proj, batch, seq)
        o_r = _hgrn(proj, r_lower, r_norm_g[l][None, :], batch, seq, l)
        w_router = jnp.zeros((d, LANES), F32).at[:, :N_GROUPS].set(w_rg[l]) \
            .at[:, N_GROUPS:N_GROUPS + N_EXPERTS].set(w_re[l])
        b_router = jnp.zeros((1, LANES), F32).at[0, :N_GROUPS].set(b_rg[l]) \
            .at[0, N_GROUPS:N_GROUPS + N_EXPERTS].set(b_re[l])
        w_router_top = _bf16_part(w_router)
        w_router = jnp.concatenate([w_router_top, w_router - w_router_top], axis=-1).astype(BF16)
        x1, h2, logits = _merge(att, o_r, proj, x2, gt1, w_up_a[l].astype(BF16), w_up_r[l].astype(BF16),
                                w_out[l].astype(BF16), norm2_g[l][None, :], sc2, sh2, w_router, b_router, seq)
        rt, cnt = _route(logits)
        dest, row_tok, blk_e, nused, run_id, next_e = _dispatch_tables(rt, cnt[0])
        yr = _moe(row_tok, blk_e, nused, run_id, next_e, h2, w1[l], w3[l], w2[l])
        x2 = _final(dest, yr, x1, rt, gt2, final_g[None, :], seq, l == depth - 1)
    return x2.reshape(batch, seq, d)
```

```python
import functools

import jax
import jax.numpy as jnp
from jax import lax
from jax.experimental import pallas as pl
from jax.experimental.pallas import tpu as pltpu

F32 = jnp.float32
BF16 = jnp.bfloat16
HIGHEST = lax.Precision.HIGHEST

D_MODEL = 2048
A_HEADS = 8
A_HEAD_DIM = 128
A_WIDTH = A_HEADS * A_HEAD_DIM
MOBA_BLOCK = 256
MOBA_TOPK = 3
R_HEADS = 8
R_KEY_DIM = 128
R_VAL_DIM = 128
R_WIDTH = R_HEADS * R_KEY_DIM
N_GROUPS = 4
EXPERTS_PER_GROUP = 8
N_EXPERTS = N_GROUPS * EXPERTS_PER_GROUP
TOP_K_INNER = 2
D_EXPERT = 512
MOE_BLOCK = 256
EPS = 1e-6
IN_COLS = 3 * A_WIDTH + 4 * R_WIDTH + 2 * D_MODEL

LANES = 128
SUBLANES = 8
CB_AQ = 0
CB_AK = CB_AQ + A_WIDTH // LANES
CB_AV = CB_AK + A_WIDTH // LANES
CB_RQ = CB_AV + A_WIDTH // LANES
CB_RF = CB_RQ + R_WIDTH // LANES
CB_RI = CB_RF + R_WIDTH // LANES
CB_ROG = CB_RI + R_WIDTH // LANES
CB_GA = CB_ROG + R_WIDTH // LANES
CB_GR = CB_GA + D_MODEL // LANES

MOBA_GROUP = 4
Q_PRESCALE = A_HEAD_DIM ** -0.5 * 1.4426950408889634
HGRN_TILE = 512
HGRN_HEADS_PER_STEP = 4
HGRN_CHUNK = 64
HGRN_SUB = 8
GATHER_LOOKAHEAD = 3
NEG_BIG = -1e30
VMEM_LIMIT = 56 * 1024 * 1024


def _cparams(sem):
    return pltpu.CompilerParams(dimension_semantics=sem, vmem_limit_bytes=VMEM_LIMIT)


def _dot_nt(a, b, **kw):
    return lax.dot_general(a, b, (((1,), (1,)), ((), ())), preferred_element_type=F32, **kw)


def _bf16_part(a):
    bits = lax.bitcast_convert_type(a, jnp.uint32) & jnp.uint32(0xFFFF0000)
    return lax.bitcast_convert_type(bits, F32)


def _dot_tn(a, b):
    return lax.dot_general(a, b, (((0,), (0,)), ((), ())), preferred_element_type=F32)


def _ada_kernel(c_ref, w_ref, b_ref, o_ref):
    c = c_ref[...]
    ca = c * jax.nn.sigmoid(c)
    w = w_ref[...]
    ca_top, w_top = _bf16_part(ca), _bf16_part(w)
    ca_hi, ca_lo = ca_top.astype(BF16), (ca - ca_top).astype(BF16)
    w_hi, w_lo = w_top.astype(BF16), (w - w_top).astype(BF16)
    acc = jnp.dot(ca_hi, w_hi, preferred_element_type=F32) + jnp.dot(ca_lo, w_hi, preferred_element_type=F32)
    o_ref[...] = acc + jnp.dot(ca_hi, w_lo, preferred_element_type=F32) + b_ref[...]


def _ada(c_pad, w_ada, b_ada):
    rows, d = c_pad.shape
    n = w_ada.shape[1]
    tn = 1024
    return pl.pallas_call(
        _ada_kernel,
        grid=(n // tn,),
        in_specs=[pl.BlockSpec((rows, d), lambda j: (0, 0)),
                  pl.BlockSpec((d, tn), lambda j: (0, j)),
                  pl.BlockSpec((1, tn), lambda j: (0, j))],
        out_specs=pl.BlockSpec((rows, tn), lambda j: (0, j)),
        out_shape=jax.ShapeDtypeStruct((rows, n), F32),
        compiler_params=_cparams(("arbitrary",)),
        name="ada",
    )(c_pad, w_ada, b_ada)


def _proj_kernel(x_ref, g_ref, sc_ref, sh_ref, w_ref, cs_ref, o_ref, h_scr):
    @pl.when(pl.program_id(1) == 0)
    def _():
        x = x_ref[...]
        ms = jnp.mean(x * x, axis=-1, keepdims=True)
        y = x * lax.rsqrt(ms + EPS) * g_ref[...]
        h_scr[...] = (y * (1.0 + sc_ref[0]) + sh_ref[0]).astype(BF16)

    acc = jnp.dot(h_scr[...], w_ref[...].astype(BF16), preferred_element_type=F32)
    o_ref[...] = (acc * cs_ref[...]).astype(o_ref.dtype)


def _proj(x2, g, sc, sh, w_in, seq):
    t, d = x2.shape
    n = w_in.shape[1]
    tm = min(1024, seq)
    tn = 1024
    bpt = seq // tm
    col_scale = jnp.ones((1, n), F32).at[:, CB_AQ * LANES:CB_AK * LANES].set(Q_PRESCALE)
    return pl.pallas_call(
        _proj_kernel,
        grid=(t // tm, n // tn),
        in_specs=[pl.BlockSpec((tm, d), lambda i, j: (i, 0)),
                  pl.BlockSpec((1, d), lambda i, j: (0, 0)),
                  pl.BlockSpec((1, 1, d), lambda i, j: (i // bpt, 0, 0)),
                  pl.BlockSpec((1, 1, d), lambda i, j: (i // bpt, 0, 0)),
                  pl.BlockSpec((d, tn), lambda i, j: (0, j)),
                  pl.BlockSpec((1, tn), lambda i, j: (0, j))],
        out_specs=pl.BlockSpec((tm, tn), lambda i, j: (i, j)),
        out_shape=jax.ShapeDtypeStruct((t, n), BF16),
        scratch_shapes=[pltpu.VMEM((tm, d), BF16)],
        compiler_params=_cparams(("arbitrary", "arbitrary")),
        name="proj",
    )(x2, g, sc, sh, w_in, col_scale)


def _moba_kernel(q_ref, k_ref, v_ref, o_ref, kaug_scr, vt_scr, kmean_scr, s_scr, *, nb, gr):
    blk = MOBA_BLOCK
    gk = MOBA_GROUP * blk
    seq = k_ref.shape[0]

    k = k_ref[...]
    kaug_scr[:, :LANES] = k
    row_blk = lax.broadcasted_iota(jnp.int32, (seq, LANES), 0) // blk
    lane = lax.broadcasted_iota(jnp.int32, (seq, LANES), 1)
    kaug_scr[:, LANES:] = jnp.where(lane == row_blk, 1.0, 0.0).astype(BF16)
    kmean_scr[...] = jnp.zeros_like(kmean_scr)
    kmean_scr[0:nb, :] = jnp.mean(k.astype(F32).reshape(nb, blk, LANES), axis=1)
    for j in range(nb):
        vj = v_ref[j * blk:(j + 1) * blk, :].astype(F32)
        vt_scr[0:LANES, j * blk:(j + 1) * blk] = vj.T.astype(BF16)
    vt_scr[LANES:LANES + 16, :] = jnp.where(
        lax.broadcasted_iota(jnp.int32, (16, seq), 0) == 0, 1.0, 0.0).astype(BF16)

    rid = lax.broadcasted_iota(jnp.int32, (gr, blk), 0)

    def scores(i):
        q = q_ref[i * blk:(i + 1) * blk, :]
        gate = _dot_nt(kmean_scr[...], q.astype(F32), precision=HIGHEST)
        gate = jnp.where(rid < i, gate, -jnp.inf)
        sel = rid == i
        for _ in range(MOBA_TOPK):
            m = jnp.max(gate, axis=0, keepdims=True)
            idx = jnp.min(jnp.where(gate == m, rid, gr), axis=0, keepdims=True)
            pick = rid == idx
            sel = sel | (pick & (rid < i))
            gate = jnp.where(pick, -jnp.inf, gate)
        bias_t = jnp.where(sel, 0.0, NEG_BIG)
        bias_t = jnp.concatenate([bias_t, jnp.zeros((LANES - gr, blk), F32)], axis=0)
        q_aug = jnp.concatenate([q, bias_t.T.astype(BF16)], axis=-1)
        n = (i + 1) * blk
        spans = [(r0, min(r0 + gk, n)) for r0 in range(0, n, gk)]
        slot = i % 2
        mx = jnp.full((8, blk), NEG_BIG, F32)
        for r0, r1 in spans:
            s = _dot_nt(kaug_scr[r0:r1, :], q_aug)
            if r1 == n:
                kpos = r0 + lax.broadcasted_iota(jnp.int32, (r1 - r0, blk), 0)
                qpos = i * blk + lax.broadcasted_iota(jnp.int32, (r1 - r0, blk), 1)
                s = jnp.where(kpos <= qpos, s, NEG_BIG)
            s_scr[slot, r0:r1, :] = s
            mx = jnp.maximum(mx, jnp.max(s.reshape((r1 - r0) // 8, 8, blk), axis=0))
        return jnp.max(mx, axis=0, keepdims=True), spans

    def weighted_values(i, m, spans):
        slot = i % 2
        acc = jnp.zeros((LANES + 16, blk), F32)
        for r0, r1 in spans:
            p = jnp.exp2(s_scr[slot, r0:r1, :] - m).astype(BF16)
            acc = acc + jnp.dot(vt_scr[:, r0:r1], p, preferred_element_type=F32)
        out_t = acc[0:LANES] / acc[LANES:LANES + 1]
        o_ref[i * blk:(i + 1) * blk, :] = out_t.T.astype(o_ref.dtype)

    pending = scores(0)
    for i in range(nb):
        nxt = scores(i + 1) if i + 1 < nb else None
        weighted_values(i, *pending)
        pending = nxt


def _moba(proj, batch, seq):
    nb = seq // MOBA_BLOCK
    blk = MOBA_BLOCK
    gr = -(-nb // 8) * 8
    t = batch * seq
    assert gr <= LANES
    return pl.pallas_call(
        functools.partial(_moba_kernel, nb=nb, gr=gr),
        grid=(batch, A_HEADS),
        in_specs=[pl.BlockSpec((seq, LANES), lambda b, h: (b, CB_AQ + h)),
                  pl.BlockSpec((seq, LANES), lambda b, h: (b, CB_AK + h)),
                  pl.BlockSpec((seq, LANES), lambda b, h: (b, CB_AV + h))],
        out_specs=pl.BlockSpec((seq, LANES), lambda b, h: (b, h)),
        out_shape=jax.ShapeDtypeStruct((t, A_WIDTH), BF16),
        scratch_shapes=[pltpu.VMEM((seq, 2 * LANES), BF16),
                        pltpu.VMEM((LANES + 16, seq), BF16),
                        pltpu.VMEM((gr, LANES), F32),
                        pltpu.VMEM((2, seq, blk), F32)],
        compiler_params=_cparams(("arbitrary", "arbitrary")),
        name="moba",
    )(proj, proj, proj)


def _hgrn_kernel(q_ref, f_ref, i_ref, og_ref, rl_ref, g_ref, o_ref, st_scr, b_scr, *, layer):
    @pl.when(pl.program_id(2) == 0)
    def _():
        st_scr[...] = jnp.zeros_like(st_scr)

    for hh in range(q_ref.shape[1] // LANES):
        cs = slice(hh * LANES, (hh + 1) * LANES)
        _hgrn_head(q_ref.at[:, cs], f_ref.at[:, cs], i_ref.at[:, cs], og_ref.at[:, cs], rl_ref.at[:, cs],
                   g_ref, o_ref.at[:, cs], st_scr.at[hh], b_scr.at[hh], layer=layer)


def _hgrn_head(q_ref, f_ref, i_ref, og_ref, rl_ref, g_ref, o_ref, st_scr, b_scr, *, layer):
    tt = q_ref.shape[0]
    ch, sub = HGRN_CHUNK, HGRN_SUB
    ns, nc = ch // sub, tt // ch
    assert SUBLANES % sub == 0 and ch % SUBLANES == 0 and tt % ch == 0

    rl = rl_ref[...]
    e = jnp.exp(rl - jnp.max(rl, axis=0, keepdims=True))
    lb = jnp.sum(e[: layer + 1], axis=0, keepdims=True) / jnp.sum(e, axis=0, keepdims=True)

    q = q_ref[...].astype(F32)
    x = f_ref[...].astype(F32)
    v_bf = i_ref[...]
    v = v_bf.astype(F32)
    f = lb + (1.0 - lb) * jax.nn.sigmoid(x)
    kin = (1.0 - lb) * jax.nn.sigmoid(-x)
    row = lax.broadcasted_iota(jnp.int32, (tt, LANES), 0)
    pic = row % ch
    pos = row % sub

    def roll8(a, k):
        return pltpu.roll(a.reshape(tt // SUBLANES, SUBLANES, LANES), k, 1).reshape(tt, LANES)

    b = jnp.log2(f)
    pos8 = row % SUBLANES
    step = 1
    while step < SUBLANES:
        b = b + jnp.where(pos8 >= step, roll8(b, step), 0.0)
        step *= 2
    gpc = ch // SUBLANES
    b4 = b.reshape(nc, gpc, SUBLANES, LANES)
    offs = [jnp.zeros((nc, 1, 1, LANES), F32)]
    for g in range(1, gpc):
        offs.append(offs[-1] + b4[:, g - 1:g, SUBLANES - 1:SUBLANES, :])
    b = (b4 + jnp.concatenate(offs, axis=1)).reshape(tt, LANES)
    b_scr[...] = b

    r_sub = b
    for k in range(1, sub):
        r_sub = jnp.where(pos == sub - 1 - k, roll8(b, SUBLANES - k), r_sub)
    kh = kin * jnp.exp2(r_sub - b)

    diag = jnp.sum(q * kin, axis=-1, keepdims=True) * v
    for delta in range(1, sub):
        kr = roll8(kin, delta)
        br = roll8(b, delta)
        vr = roll8(v, delta)
        w = jnp.where(pos >= delta, q * kr * jnp.exp2(b - br), 0.0)
        diag = diag + jnp.sum(w, axis=-1, keepdims=True) * vr

    pc = lax.broadcasted_iota(jnp.int32, (ch, LANES), 0)
    sid = pc // sub
    st = st_scr[...]
    outs = []
    for c in range(nc):
        sl = slice(c * ch, (c + 1) * ch)
        q_c, b_c, kh_c, vc = q[sl], b[sl], kh[sl], v_bf[sl]
        q_blocks, k_blocks = [], []
        for j in range(ns - 1):
            rj = b_scr[pl.ds(c * ch + sub * (j + 1) - 1, 1), :]
            qj = q_c * jnp.exp2(jnp.where(pc >= sub * (j + 1), b_c - rj, -jnp.inf))
            q_blocks.append(qj.astype(BF16))
            k_blocks.append(jnp.where(sid == j, kh_c, 0.0).astype(BF16))
        a_off = _dot_nt(jnp.concatenate(q_blocks, axis=-1), jnp.concatenate(k_blocks, axis=-1))
        o_c = jnp.dot(a_off.astype(BF16), vc, preferred_element_type=F32)
        o_c = o_c + _dot_nt((q_c * jnp.exp2(b_c)).astype(BF16), st.astype(BF16)) + diag[sl]
        r_last = b_scr[pl.ds((c + 1) * ch - 1, 1), :]
        ke = (kh_c * jnp.exp2(r_last - r_sub[sl])).astype(BF16)
        st = jnp.exp2(r_last) * st + _dot_tn(vc, ke)
        outs.append(o_c)
    st_scr[...] = st
    o = jnp.concatenate(outs, axis=0)

    y = o * lax.rsqrt(jnp.mean(o * o, axis=-1, keepdims=True) + EPS) * g_ref[...]
    og = og_ref[...].astype(F32)
    o_ref[...] = (y * (og * jax.nn.sigmoid(og))).astype(o_ref.dtype)


def _hgrn(proj, r_lower, r_norm_g, batch, seq, layer):
    tt = min(HGRN_TILE, seq)
    nt = seq // tt
    t = batch * seq
    nl = r_lower.shape[0]

    nh = HGRN_HEADS_PER_STEP
    hw = nh * LANES
    assert R_HEADS % nh == 0 and all(cb % nh == 0 for cb in (CB_RQ, CB_RF, CB_RI, CB_ROG))

    def col(cb):
        return pl.BlockSpec((tt, hw), lambda b, h, c: (b * nt + c, cb // nh + h))

    return pl.pallas_call(
        functools.partial(_hgrn_kernel, layer=layer),
        grid=(batch, R_HEADS // nh, nt),
        in_specs=[col(CB_RQ), col(CB_RF), col(CB_RI), col(CB_ROG),
                  pl.BlockSpec((nl, hw), lambda b, h, c: (0, h)),
                  pl.BlockSpec((1, LANES), lambda b, h, c: (0, 0))],
        out_specs=pl.BlockSpec((tt, hw), lambda b, h, c: (b * nt + c, h)),
        out_shape=jax.ShapeDtypeStruct((t, R_WIDTH), BF16),
        scratch_shapes=[pltpu.VMEM((nh, LANES, LANES), F32), pltpu.VMEM((nh, tt, LANES), F32)],
        compiler_params=_cparams(("arbitrary", "arbitrary", "arbitrary")),
        name="hgrn",
    )(proj, proj, proj, proj, r_lower, r_norm_g)


def _merge_kernel(att_ref, or_ref, ga0_ref, ga1_ref, gr0_ref, gr1_ref, x_ref, gt_ref,
                  wua_ref, wur_ref, wo_ref, g2_ref, sc_ref, sh_ref, wr_ref, br_ref,
                  x1_ref, h2_ref, lg_ref):
    ya = jnp.dot(att_ref[...], wua_ref[...], preferred_element_type=F32)
    yr = jnp.dot(or_ref[...], wur_ref[...], preferred_element_type=F32)
    ga = jnp.concatenate([ga0_ref[...], ga1_ref[...]], axis=-1).astype(F32)
    gr = jnp.concatenate([gr0_ref[...], gr1_ref[...]], axis=-1).astype(F32)
    merged = jax.nn.sigmoid(ga) * ya + jax.nn.sigmoid(gr) * yr
    out = jnp.dot(merged.astype(BF16), wo_ref[...], preferred_element_type=F32)
    x1 = x_ref[...] + gt_ref[0] * out
    x1_ref[...] = x1
    ms = jnp.mean(x1 * x1, axis=-1, keepdims=True)
    h2 = (x1 * lax.rsqrt(ms + EPS) * g2_ref[...]) * (1.0 + sc_ref[0]) + sh_ref[0]
    h2_ref[...] = h2
    h2_top = _bf16_part(h2)
    h2_hi = h2_top.astype(BF16)
    h2_lo = (h2 - h2_top).astype(BF16)
    wr = wr_ref[...]
    p_hi = jnp.dot(h2_hi, wr, preferred_element_type=F32)
    p_lo = jnp.dot(h2_lo, wr[:, :LANES], preferred_element_type=F32)
    lg_ref[...] = p_hi[:, :LANES] + p_hi[:, LANES:] + p_lo + br_ref[...]


def _merge(att, o_r, proj, x2, gt1, w_up_a, w_up_r, w_out, g2, sc2, sh2, w_router, b_router, seq):
    t, d = x2.shape
    tm = 256
    bpt = seq // tm
    half = d // 2
    cb = half // LANES

    def gspec(cb0, k):
        return pl.BlockSpec((tm, half), lambda i: (i, cb0 // cb + k))

    def const(shape):
        return pl.BlockSpec(shape, lambda i: tuple(0 for _ in shape), pipeline_mode=pl.Buffered(1))

    def per_batch():
        return pl.BlockSpec((1, 1, d), lambda i: (i // bpt, 0, 0))

    return pl.pallas_call(
        _merge_kernel,
        grid=(t // tm,),
        in_specs=[pl.BlockSpec((tm, A_WIDTH), lambda i: (i, 0)),
                  pl.BlockSpec((tm, R_WIDTH), lambda i: (i, 0)),
                  gspec(CB_GA, 0), gspec(CB_GA, 1), gspec(CB_GR, 0), gspec(CB_GR, 1),
                  pl.BlockSpec((tm, d), lambda i: (i, 0)),
                  per_batch(),
                  const((A_WIDTH, d)), const((R_WIDTH, d)), const((d, d)),
                  const((1, d)), per_batch(), per_batch(),
                  const((d, 2 * LANES)), const((1, LANES))],
        out_specs=[pl.BlockSpec((tm, d), lambda i: (i, 0)),
                   pl.BlockSpec((tm, d), lambda i: (i, 0)),
                   pl.BlockSpec((tm, LANES), lambda i: (i, 0))],
        out_shape=[jax.ShapeDtypeStruct((t, d), F32),
                   jax.ShapeDtypeStruct((t, d), F32),
                   jax.ShapeDtypeStruct((t, LANES), F32)],
        compiler_params=_cparams(("arbitrary",)),
        name="merge",
    )(att, o_r, proj, proj, proj, proj, x2, gt1, w_up_a, w_up_r, w_out, g2, sc2, sh2, w_router, b_router)


RT_E0, RT_E1, RT_W0, RT_W1, RT_R0, RT_R1 = 0, 1, 2, 3, 4, 5


def _route_kernel(lg_ref, rt_ref, cnt_ref, carry_scr):
    tr = lg_ref.shape[0]

    @pl.when(pl.program_id(0) == 0)
    def _():
        carry_scr[...] = jnp.zeros_like(carry_scr)

    x = lg_ref[...]
    lane = lax.broadcasted_iota(jnp.int32, (tr, LANES), 1)
    ninf = -jnp.inf

    def lane_max(val):
        return jnp.max(val, axis=-1, keepdims=True)

    def first_lane(mask):
        return jnp.min(jnp.where(mask, lane, LANES), axis=-1, keepdims=True)

    is_g = lane < N_GROUPS
    gmax = lane_max(jnp.where(is_g, x, ninf))
    grp = first_lane(is_g & (x == gmax))
    eg = jnp.where(is_g, jnp.exp(x - gmax), 0.0)
    pg_top = 1.0 / jnp.sum(eg, axis=-1, keepdims=True)

    lo = N_GROUPS + grp * EXPERTS_PER_GROUP
    is_e = (lane >= lo) & (lane < lo + EXPERTS_PER_GROUP)
    emax = lane_max(jnp.where(is_e, x, ninf))
    ee = jnp.where(is_e, jnp.exp(x - emax), 0.0)
    pe = ee / jnp.sum(ee, axis=-1, keepdims=True)
    pe = jnp.where(is_e, pe, ninf)
    p0 = lane_max(pe)
    l0 = first_lane(pe == p0)
    pe1 = jnp.where(lane == l0, ninf, pe)
    p1 = lane_max(pe1)
    l1 = first_lane(pe1 == p1)
    den = p0 + p1
    w0 = pg_top * p0 / den
    w1 = pg_top * p1 / den
    e0 = l0 - N_GROUPS
    e1 = l1 - N_GROUPS

    onehot = ((lane == e0) | (lane == e1)).astype(BF16)
    r = lax.broadcasted_iota(jnp.int32, (tr, tr), 0)
    c = lax.broadcasted_iota(jnp.int32, (tr, tr), 1)
    tri = jnp.where(c < r, 1.0, 0.0).astype(BF16)
    prefix = jnp.dot(tri, onehot, preferred_element_type=F32) + carry_scr[0:1, :]
    rank0 = jnp.sum(jnp.where(lane == e0, prefix, 0.0), axis=-1, keepdims=True)
    rank1 = jnp.sum(jnp.where(lane == e1, prefix, 0.0), axis=-1, keepdims=True)
    total = carry_scr[0:1, :] + jnp.sum(onehot.astype(F32), axis=0, keepdims=True)
    carry_scr[...] = jnp.broadcast_to(total, carry_scr.shape)
    cnt_ref[...] = jnp.broadcast_to(total, cnt_ref.shape)

    rec = jnp.zeros((tr, LANES), F32)
    for k, val in ((RT_E0, e0.astype(F32)), (RT_E1, e1.astype(F32)), (RT_W0, w0), (RT_W1, w1),
                   (RT_R0, rank0), (RT_R1, rank1)):
        rec = jnp.where(lane == k, val, rec)
    rt_ref[...] = rec


def _route(logits):
    t = logits.shape[0]
    tr = 512
    return pl.pallas_call(
        _route_kernel,
        grid=(t // tr,),
        in_specs=[pl.BlockSpec((tr, LANES), lambda i: (i, 0))],
        out_specs=[pl.BlockSpec((tr, LANES), lambda i: (i, 0)),
                   pl.BlockSpec((8, LANES), lambda i: (0, 0))],
        out_shape=[jax.ShapeDtypeStruct((t, LANES), F32),
                   jax.ShapeDtypeStruct((8, LANES), F32)],
        scratch_shapes=[pltpu.VMEM((8, LANES), F32)],
        compiler_params=_cparams(("arbitrary",)),
        name="route",
    )(logits)


def _row_copy(src_hbm, dst_buf, src_row, dst_row, sem):
    return pltpu.make_async_copy(src_hbm.at[pl.ds(src_row, 1), :], dst_buf.at[pl.ds(dst_row, 1), :], sem)


def _wait_rows(src_hbm, dst_buf, n_rows, sem):
    def body(r, _):
        _row_copy(src_hbm, dst_buf, 0, r, sem).wait()
        return 0

    lax.fori_loop(0, n_rows, body, 0, unroll=8)


def _moe_kernel(tok_ref, be_ref, nu_ref, run_ref, nxt_ref, h_hbm, w1_hbm, w3_hbm, w2_hbm, y_ref, *scratch):
    depth = GATHER_LOOKAHEAD
    bufs = scratch[:depth + 1]
    sem, w1_f, w3_f, w2_f, wsem, w1_bf, w3_bf, w2_bf = scratch[depth + 1:]
    rows = MOE_BLOCK
    i = pl.program_id(0)
    nused = nu_ref[0]

    def issue(block, parity):
        base = block * rows
        for r in range(rows):
            _row_copy(h_hbm, bufs[parity], tok_ref[base + r], r, sem.at[parity]).start()

    for first in range(depth):
        @pl.when((i == 0) & (first < nused))
        def _(first=first):
            issue(first, first)

    def weight_copies(expert, slot):
        return [pltpu.make_async_copy(src.at[expert], dst.at[slot], wsem.at[slot])
                for src, dst in ((w1_hbm, w1_f), (w3_hbm, w3_f), (w2_hbm, w2_f))]

    new_expert = (i == 0) | (be_ref[i] != be_ref[jnp.maximum(i - 1, 0)])
    wslot = run_ref[i] % 2

    @pl.when((i == 0) & (nused > 0))
    def _():
        for cp in weight_copies(be_ref[0], 0):
            cp.start()

    @pl.when((i < nused) & new_expert)
    def _():
        for cp in weight_copies(be_ref[i], wslot):
            cp.wait()

        @pl.when(nxt_ref[i] >= 0)
        def _():
            for cp in weight_copies(nxt_ref[i], 1 - wslot):
                cp.start()

        w1_bf[...] = w1_f[wslot].astype(BF16)
        w3_bf[...] = w3_f[wslot].astype(BF16)
        w2_bf[...] = w2_f[wslot].astype(BF16)

    def block_step(parity, prefetch):
        _wait_rows(h_hbm, bufs[parity], rows, sem.at[parity])
        if prefetch:
            issue(i + depth, (parity + depth) % len(bufs))
        xb = bufs[parity][...].astype(BF16)
        h1 = jnp.dot(xb, w1_bf[...], preferred_element_type=F32)
        h3 = jnp.dot(xb, w3_bf[...], preferred_element_type=F32)
        hid = (h1 * jax.nn.sigmoid(h1)) * h3
        y_ref[...] = jnp.dot(hid.astype(BF16), w2_bf[...], preferred_element_type=F32)

    for parity in range(len(bufs)):
        for prefetch in (True, False):
            more = (i + depth < nused) if prefetch else ((i < nused) & (i + depth >= nused))
            pl.when((i % len(bufs) == parity) & more)(functools.partial(block_step, parity, prefetch))

    @pl.when(i >= nused)
    def _():
        y_ref[...] = jnp.zeros_like(y_ref)


def _moe(row_tok, blk_e, nused, run_id, next_e, h2, w1, w3, w2):
    t, d = h2.shape
    r = row_tok.shape[0]
    nblk = r // MOE_BLOCK
    grid_spec = pltpu.PrefetchScalarGridSpec(
        num_scalar_prefetch=5,
        grid=(nblk,),
        in_specs=[pl.BlockSpec(memory_space=pl.ANY)] * 4,
        out_specs=pl.BlockSpec((MOE_BLOCK, d), lambda i, *_: (i, 0)),
        scratch_shapes=[pltpu.VMEM((MOE_BLOCK, d), F32)] * (GATHER_LOOKAHEAD + 1) + [
                        pltpu.SemaphoreType.DMA((GATHER_LOOKAHEAD + 1,)),
                        pltpu.VMEM((2, d, D_EXPERT), F32), pltpu.VMEM((2, d, D_EXPERT), F32),
                        pltpu.VMEM((2, D_EXPERT, d), F32), pltpu.SemaphoreType.DMA((2,)),
                        pltpu.VMEM((d, D_EXPERT), BF16), pltpu.VMEM((d, D_EXPERT), BF16),
                        pltpu.VMEM((D_EXPERT, d), BF16)],
    )
    return pl.pallas_call(
        _moe_kernel,
        grid_spec=grid_spec,
        out_shape=jax.ShapeDtypeStruct((r, d), F32),
        compiler_params=_cparams(("arbitrary",)),
        name="moe",
    )(row_tok, blk_e, nused, run_id, next_e, h2, w1, w3, w2)


def _final_kernel(dest_ref, y_hbm, x1_ref, rt_ref, gt_ref, g_ref, o_ref, *scratch, last_layer):
    depth = GATHER_LOOKAHEAD
    bufs, sem = scratch[:depth + 1], scratch[depth + 1]
    tf = x1_ref.shape[0]
    i = pl.program_id(0)
    n = pl.num_programs(0)

    def issue(step, parity):
        base = step * tf * TOP_K_INNER
        for r in range(tf):
            for k in range(TOP_K_INNER):
                _row_copy(y_hbm, bufs[parity].at[k], dest_ref[base + r * TOP_K_INNER + k], r,
                          sem.at[parity]).start()

    for first in range(depth):
        @pl.when((i == 0) & (first < n))
        def _(first=first):
            issue(first, first)

    def tile_step(parity, prefetch):
        buf = bufs[parity]
        for k in range(TOP_K_INNER):
            _wait_rows(y_hbm, buf.at[k], tf, sem.at[parity])
        if prefetch:
            issue(i + depth, (parity + depth) % len(bufs))
        rt = rt_ref[...]
        w0 = rt[:, RT_W0:RT_W0 + 1]
        w1 = rt[:, RT_W1:RT_W1 + 1]
        y = buf[0] * w0 + buf[1] * w1
        x2 = x1_ref[...] + gt_ref[0] * y
        if last_layer:
            ms = jnp.mean(x2 * x2, axis=-1, keepdims=True)
            x2 = x2 * lax.rsqrt(ms + EPS) * g_ref[...]
        o_ref[...] = x2

    for parity in range(len(bufs)):
        for prefetch in (True, False):
            more = (i + depth < n) if prefetch else (i + depth >= n)
            pl.when((i % len(bufs) == parity) & more)(functools.partial(tile_step, parity, prefetch))


def _final(dest, yr, x1, rt, gt2, final_g, seq, last_layer):
    t, d = x1.shape
    tf = 256
    bpt = seq // tf
    grid_spec = pltpu.PrefetchScalarGridSpec(
        num_scalar_prefetch=1,
        grid=(t // tf,),
        in_specs=[pl.BlockSpec(memory_space=pl.ANY),
                  pl.BlockSpec((tf, d), lambda i, dst: (i, 0)),
                  pl.BlockSpec((tf, LANES), lambda i, dst: (i, 0)),
                  pl.BlockSpec((1, 1, d), lambda i, dst: (i // bpt, 0, 0)),
                  pl.BlockSpec((1, d), lambda i, dst: (0, 0))],
        out_specs=pl.BlockSpec((tf, d), lambda i, dst: (i, 0)),
        scratch_shapes=[pltpu.VMEM((TOP_K_INNER, tf, d), F32)] * (GATHER_LOOKAHEAD + 1) + [
                        pltpu.SemaphoreType.DMA((GATHER_LOOKAHEAD + 1,))],
    )
    return pl.pallas_call(
        functools.partial(_final_kernel, last_layer=last_layer),
        grid_spec=grid_spec,
        out_shape=jax.ShapeDtypeStruct((t, d), F32),
        compiler_params=_cparams(("arbitrary",)),
        name="final",
    )(dest, yr, x1, rt, gt2, final_g)


def _dispatch_tables(rt, counts_row):
    t = rt.shape[0]
    tk = t * TOP_K_INNER
    eid = rt[:, RT_E0:RT_E1 + 1].astype(jnp.int32)
    rank = rt[:, RT_R0:RT_R1 + 1].astype(jnp.int32)
    counts = counts_row[:N_EXPERTS].astype(jnp.int32)
    padded = (counts + MOE_BLOCK - 1) // MOE_BLOCK * MOE_BLOCK
    pad_end = jnp.cumsum(padded)
    pad_start = pad_end - padded
    onehot = (eid[..., None] == jnp.arange(N_EXPERTS, dtype=jnp.int32)).astype(F32)
    base = jnp.einsum('tke,e->tk', onehot, pad_start.astype(F32), precision=HIGHEST)
    dest = (base.astype(jnp.int32) + rank).reshape(tk)
    r = tk + N_EXPERTS * MOE_BLOCK
    tok = jnp.repeat(jnp.arange(t, dtype=jnp.int32), TOP_K_INNER)
    row_tok = jnp.zeros((r,), jnp.int32).at[dest].set(tok)
    nblk = r // MOE_BLOCK
    blk_start = jnp.arange(nblk, dtype=jnp.int32) * MOE_BLOCK
    blk_e = jnp.minimum(jnp.sum(pad_end[None, :] <= blk_start[:, None], axis=1), N_EXPERTS - 1).astype(jnp.int32)
    nused = (pad_end[-1:] // MOE_BLOCK).astype(jnp.int32)
    idx = jnp.arange(nblk, dtype=jnp.int32)
    is_new = jnp.concatenate([jnp.ones((1,), bool), blk_e[1:] != blk_e[:-1]])
    run_id = (jnp.cumsum(is_new.astype(jnp.int32)) - 1).astype(jnp.int32)
    later = (idx[None, :] > idx[:, None]) & (blk_e[None, :] != blk_e[:, None]) & (idx[None, :] < nused[0])
    first_later = jnp.min(jnp.where(later, idx[None, :], nblk), axis=1)
    next_e = jnp.where(first_later < nblk, blk_e[jnp.minimum(first_later, nblk - 1)], -1).astype(jnp.int32)
    return dest, row_tok, blk_e, nused, run_id, next_e


def kernel(x, c, norm1_g, norm2_g, final_g, w_ada, b_ada, w_in, r_lower, r_norm_g,
           w_up_a, w_up_r, w_out, w_rg, b_rg, w_re, b_re, w1, w3, w2):
    batch, seq, d = x.shape
    t = batch * seq
    depth = w_in.shape[0]
    x2 = x.reshape(t, d)
    c_pad = jnp.zeros((-(-batch // 16) * 16, d), F32).at[:batch].set(c)
    for l in range(depth):
        mod = _ada(c_pad, w_ada[l], b_ada[l][None, :])[:batch]
        sh1, sc1, gt1, sh2, sc2, gt2 = [m[:, None, :] for m in jnp.split(mod, 6, axis=-1)]
        proj = _proj(x2, norm1_g[l][None, :], sc1, sh1, w_in[l], seq)
        att = _moba(proj, batch, seq)
        o_r = _hgrn(proj, r_lower, r_norm_g[l][None, :], batch, seq, l)
        w_router = jnp.zeros((d, LANES), F32).at[:, :N_GROUPS].set(w_rg[l]) \
            .at[:, N_GROUPS:N_GROUPS + N_EXPERTS].set(w_re[l])
        b_router = jnp.zeros((1, LANES), F32).at[0, :N_GROUPS].set(b_rg[l]) \
            .at[0, N_GROUPS:N_GROUPS + N_EXPERTS].set(b_re[l])
        w_router_top = _bf16_part(w_router)
        w_router = jnp.concatenate([w_router_top, w_router - w_router_top], axis=-1).astype(BF16)
        x1, h2, logits = _merge(att, o_r, proj, x2, gt1, w_up_a[l].astype(BF16), w_up_r[l].astype(BF16),
                                w_out[l].astype(BF16), norm2_g[l][None, :], sc2, sh2, w_router, b_router, seq)
        rt, cnt = _route(logits)
        dest, row_tok, blk_e, nused, run_id, next_e = _dispatch_tables(rt, cnt[0])
        yr = _moe(row_tok, blk_e, nused, run_id, next_e, h2, w1[l], w3[l], w2[l])
        x2 = _final(dest, yr, x1, rt, gt2, final_g[None, :], seq, l == depth - 1)
    return x2.reshape(batch, seq, d)
```

```python
import functools

import jax
import jax.numpy as jnp
from jax import lax
from jax.experimental import pallas as pl
from jax.experimental.pallas import tpu as pltpu

F32 = jnp.float32
BF16 = jnp.bfloat16
HIGHEST = lax.Precision.HIGHEST

D_MODEL = 2048
A_HEADS = 8
A_HEAD_DIM = 128
A_WIDTH = A_HEADS * A_HEAD_DIM
MOBA_BLOCK = 256
MOBA_TOPK = 3
R_HEADS = 8
R_KEY_DIM = 128
R_VAL_DIM = 128
R_WIDTH = R_HEADS * R_KEY_DIM
N_GROUPS = 4
EXPERTS_PER_GROUP = 8
N_EXPERTS = N_GROUPS * EXPERTS_PER_GROUP
TOP_K_INNER = 2
D_EXPERT = 512
MOE_BLOCK = 256
EPS = 1e-6
IN_COLS = 3 * A_WIDTH + 4 * R_WIDTH + 2 * D_MODEL

LANES = 128
SUBLANES = 8
CB_AQ = 0
CB_AK = CB_AQ + A_WIDTH // LANES
CB_AV = CB_AK + A_WIDTH // LANES
CB_RQ = CB_AV + A_WIDTH // LANES
CB_RF = CB_RQ + R_WIDTH // LANES
CB_RI = CB_RF + R_WIDTH // LANES
CB_ROG = CB_RI + R_WIDTH // LANES
CB_GA = CB_ROG + R_WIDTH // LANES
CB_GR = CB_GA + D_MODEL // LANES

MOBA_GROUP = 4
Q_PRESCALE = A_HEAD_DIM ** -0.5 * 1.4426950408889634
HGRN_TILE = 512
HGRN_HEADS_PER_STEP = 4
HGRN_CHUNK = 64
HGRN_SUB = 8
GATHER_LOOKAHEAD = 3
NEG_BIG = -1e30
VMEM_LIMIT = 56 * 1024 * 1024


def _cparams(sem):
    return pltpu.CompilerParams(dimension_semantics=sem, vmem_limit_bytes=VMEM_LIMIT)


def _dot_nt(a, b, **kw):
    return lax.dot_general(a, b, (((1,), (1,)), ((), ())), preferred_element_type=F32, **kw)


def _bf16_part(a):
    bits = lax.bitcast_convert_type(a, jnp.uint32) & jnp.uint32(0xFFFF0000)
    return lax.bitcast_convert_type(bits, F32)


def _dot_tn(a, b):
    return lax.dot_general(a, b, (((0,), (0,)), ((), ())), preferred_element_type=F32)


def _ada_kernel(c_ref, w_ref, b_ref, o_ref):
    c = c_ref[...]
    ca = c * jax.nn.sigmoid(c)
    w = w_ref[...]
    ca_top, w_top = _bf16_part(ca), _bf16_part(w)
    ca_hi, ca_lo = ca_top.astype(BF16), (ca - ca_top).astype(BF16)
    w_hi, w_lo = w_top.astype(BF16), (w - w_top).astype(BF16)
    acc = jnp.dot(ca_hi, w_hi, preferred_element_type=F32) + jnp.dot(ca_lo, w_hi, preferred_element_type=F32)
    o_ref[...] = acc + jnp.dot(ca_hi, w_lo, preferred_element_type=F32) + b_ref[...]


def _ada(c_pad, w_ada, b_ada):
    rows, d = c_pad.shape
    n = w_ada.shape[1]
    tn = 1024
    return pl.pallas_call(
        _ada_kernel,
        grid=(n // tn,),
        in_specs=[pl.BlockSpec((rows, d), lambda j: (0, 0)),
                  pl.BlockSpec((d, tn), lambda j: (0, j)),
                  pl.BlockSpec((1, tn), lambda j: (0, j))],
        out_specs=pl.BlockSpec((rows, tn), lambda j: (0, j)),
        out_shape=jax.ShapeDtypeStruct((rows, n), F32),
        compiler_params=_cparams(("arbitrary",)),
        name="ada",
    )(c_pad, w_ada, b_ada)


def _proj_kernel(x_ref, g_ref, sc_ref, sh_ref, w_ref, cs_ref, o_ref, h_scr):
    @pl.when(pl.program_id(1) == 0)
    def _():
        x = x_ref[...]
        ms = jnp.mean(x * x, axis=-1, keepdims=True)
        y = x * lax.rsqrt(ms + EPS) * g_ref[...]
        h_scr[...] = (y * (1.0 + sc_ref[0]) + sh_ref[0]).astype(BF16)

    acc = jnp.dot(h_scr[...], w_ref[...].astype(BF16), preferred_element_type=F32)
    o_ref[...] = (acc * cs_ref[...]).astype(o_ref.dtype)


def _proj(x2, g, sc, sh, w_in, seq):
    t, d = x2.shape
    n = w_in.shape[1]
    tm = min(1024, seq)
    tn = 1024
    bpt = seq // tm
    col_scale = jnp.ones((1, n), F32).at[:, CB_AQ * LANES:CB_AK * LANES].set(Q_PRESCALE)
    return pl.pallas_call(
        _proj_kernel,
        grid=(t // tm, n // tn),
        in_specs=[pl.BlockSpec((tm, d), lambda i, j: (i, 0)),
                  pl.BlockSpec((1, d), lambda i, j: (0, 0)),
                  pl.BlockSpec((1, 1, d), lambda i, j: (i // bpt, 0, 0)),
                  pl.BlockSpec((1, 1, d), lambda i, j: (i // bpt, 0, 0)),
                  pl.BlockSpec((d, tn), lambda i, j: (0, j)),
                  pl.BlockSpec((1, tn), lambda i, j: (0, j))],
        out_specs=pl.BlockSpec((tm, tn), lambda i, j: (i, j)),
        out_shape=jax.ShapeDtypeStruct((t, n), BF16),
        scratch_shapes=[pltpu.VMEM((tm, d), BF16)],
        compiler_params=_cparams(("arbitrary", "arbitrary")),
        name="proj",
    )(x2, g, sc, sh, w_in, col_scale)


def _moba_kernel(q_ref, k_ref, v_ref, o_ref, kaug_scr, vt_scr, kmean_scr, s_scr, *, nb, gr):
    blk = MOBA_BLOCK
    gk = MOBA_GROUP * blk
    seq = k_ref.shape[0]

    k = k_ref[...]
    kaug_scr[:, :LANES] = k
    row_blk = lax.broadcasted_iota(jnp.int32, (seq, LANES), 0) // blk
    lane = lax.broadcasted_iota(jnp.int32, (seq, LANES), 1)
    kaug_scr[:, LANES:] = jnp.where(lane == row_blk, 1.0, 0.0).astype(BF16)
    kmean_scr[...] = jnp.zeros_like(kmean_scr)
    kmean_scr[0:nb, :] = jnp.mean(k.astype(F32).reshape(nb, blk, LANES), axis=1)
    for j in range(nb):
        vj = v_ref[j * blk:(j + 1) * blk, :].astype(F32)
        vt_scr[0:LANES, j * blk:(j + 1) * blk] = vj.T.astype(BF16)
    vt_scr[LANES:LANES + 16, :] = jnp.where(
        lax.broadcasted_iota(jnp.int32, (16, seq), 0) == 0, 1.0, 0.0).astype(BF16)

    rid = lax.broadcasted_iota(jnp.int32, (gr, blk), 0)

    def scores(i):
        q = q_ref[i * blk:(i + 1) * blk, :]
        gate = _dot_nt(kmean_scr[...], q.astype(F32), precision=HIGHEST)
        gate = jnp.where(rid < i, gate, -jnp.inf)
        sel = rid == i
        for _ in range(MOBA_TOPK):
            m = jnp.max(gate, axis=0, keepdims=True)
            idx = jnp.min(jnp.where(gate == m, rid, gr), axis=0, keepdims=True)
            pick = rid == idx
            sel = sel | (pick & (rid < i))
            gate = jnp.where(pick, -jnp.inf, gate)
        bias_t = jnp.where(sel, 0.0, NEG_BIG)
        bias_t = jnp.concatenate([bias_t, jnp.zeros((LANES - gr, blk), F32)], axis=0)
        q_aug = jnp.concatenate([q, bias_t.T.astype(BF16)], axis=-1)
        n = (i + 1) * blk
        spans = [(r0, min(r0 + gk, n)) for r0 in range(0, n, gk)]
        slot = i % 2
        mx = jnp.full((8, blk), NEG_BIG, F32)
        for r0, r1 in spans:
            s = _dot_nt(kaug_scr[r0:r1, :], q_aug)
            if r1 == n:
                kpos = r0 + lax.broadcasted_iota(jnp.int32, (r1 - r0, blk), 0)
                qpos = i * blk + lax.broadcasted_iota(jnp.int32, (r1 - r0, blk), 1)
                s = jnp.where(kpos <= qpos, s, NEG_BIG)
            s_scr[slot, r0:r1, :] = s
            mx = jnp.maximum(mx, jnp.max(s.reshape((r1 - r0) // 8, 8, blk), axis=0))
        return jnp.max(mx, axis=0, keepdims=True), spans

    def weighted_values(i, m, spans):
        slot = i % 2
        acc = jnp.zeros((LANES + 16, blk), F32)
        for r0, r1 in spans:
            p = jnp.exp2(s_scr[slot, r0:r1, :] - m).astype(BF16)
            acc = acc + jnp.dot(vt_scr[:, r0:r1], p, preferred_element_type=F32)
        out_t = acc[0:LANES] / acc[LANES:LANES + 1]
        o_ref[i * blk:(i + 1) * blk, :] = out_t.T.astype(o_ref.dtype)

    pending = scores(0)
    for i in range(nb):
        nxt = scores(i + 1) if i + 1 < nb else None
        weighted_values(i, *pending)
        pending = nxt


def _moba(proj, batch, seq):
    nb = seq // MOBA_BLOCK
    blk = MOBA_BLOCK
    gr = -(-nb // 8) * 8
    t = batch * seq
    assert gr <= LANES
    return pl.pallas_call(
        functools.partial(_moba_kernel, nb=nb, gr=gr),
        grid=(batch, A_HEADS),
        in_specs=[pl.BlockSpec((seq, LANES), lambda b, h: (b, CB_AQ + h)),
                  pl.BlockSpec((seq, LANES), lambda b, h: (b, CB_AK + h)),
                  pl.BlockSpec((seq, LANES), lambda b, h: (b, CB_AV + h))],
        out_specs=pl.BlockSpec((seq, LANES), lambda b, h: (b, h)),
        out_shape=jax.ShapeDtypeStruct((t, A_WIDTH), BF16),
        scratch_shapes=[pltpu.VMEM((seq, 2 * LANES), BF16),
                        pltpu.VMEM((LANES + 16, seq), BF16),
                        pltpu.VMEM((gr, LANES), F32),
                        pltpu.VMEM((2, seq, blk), F32)],
        compiler_params=_cparams(("arbitrary", "arbitrary")),
        name="moba",
    )(proj, proj, proj)


def _hgrn_kernel(q_ref, f_ref, i_ref, og_ref, rl_ref, g_ref, o_ref, st_scr, b_scr, *, layer):
    @pl.when(pl.program_id(2) == 0)
    def _():
        st_scr[...] = jnp.zeros_like(st_scr)

    for hh in range(q_ref.shape[1] // LANES):
        cs = slice(hh * LANES, (hh + 1) * LANES)
        _hgrn_head(q_ref.at[:, cs], f_ref.at[:, cs], i_ref.at[:, cs], og_ref.at[:, cs], rl_ref.at[:, cs],
                   g_ref, o_ref.at[:, cs], st_scr.at[hh], b_scr.at[hh], layer=layer)


def _hgrn_head(q_ref, f_ref, i_ref, og_ref, rl_ref, g_ref, o_ref, st_scr, b_scr, *, layer):
    tt = q_ref.shape[0]
    ch, sub = HGRN_CHUNK, HGRN_SUB
    ns, nc = ch // sub, tt // ch
    assert SUBLANES % sub == 0 and ch % SUBLANES == 0 and tt % ch == 0

    rl = rl_ref[...]
    e = jnp.exp(rl - jnp.max(rl, axis=0, keepdims=True))
    lb = jnp.sum(e[: layer + 1], axis=0, keepdims=True) / jnp.sum(e, axis=0, keepdims=True)

    q = q_ref[...].astype(F32)
    x = f_ref[...].astype(F32)
    v_bf = i_ref[...]
    v = v_bf.astype(F32)
    f = lb + (1.0 - lb) * jax.nn.sigmoid(x)
    kin = (1.0 - lb) * jax.nn.sigmoid(-x)
    row = lax.broadcasted_iota(jnp.int32, (tt, LANES), 0)
    pic = row % ch
    pos = row % sub

    def roll8(a, k):
        return pltpu.roll(a.reshape(tt // SUBLANES, SUBLANES, LANES), k, 1).reshape(tt, LANES)

    b = jnp.log2(f)
    pos8 = row % SUBLANES
    step = 1
    while step < SUBLANES:
        b = b + jnp.where(pos8 >= step, roll8(b, step), 0.0)
        step *= 2
    gpc = ch // SUBLANES
    b4 = b.reshape(nc, gpc, SUBLANES, LANES)
    offs = [jnp.zeros((nc, 1, 1, LANES), F32)]
    for g in range(1, gpc):
        offs.append(offs[-1] + b4[:, g - 1:g, SUBLANES - 1:SUBLANES, :])
    b = (b4 + jnp.concatenate(offs, axis=1)).reshape(tt, LANES)
    b_scr[...] = b

    r_sub = b
    for k in range(1, sub):
        r_sub = jnp.where(pos == sub - 1 - k, roll8(b, SUBLANES - k), r_sub)
    kh = kin * jnp.exp2(r_sub - b)

    diag = jnp.sum(q * kin, axis=-1, keepdims=True) * v
    for delta in range(1, sub):
        kr = roll8(kin, delta)
        br = roll8(b, delta)
        vr = roll8(v, delta)
        w = jnp.where(pos >= delta, q * kr * jnp.exp2(b - br), 0.0)
        diag = diag + jnp.sum(w, axis=-1, keepdims=True) * vr

    pc = lax.broadcasted_iota(jnp.int32, (ch, LANES), 0)
    sid = pc // sub
    st = st_scr[...]
    outs = []
    for c in range(nc):
        sl = slice(c * ch, (c + 1) * ch)
        q_c, b_c, kh_c, vc = q[sl], b[sl], kh[sl], v_bf[sl]
        q_blocks, k_blocks = [], []
        for j in range(ns - 1):
            rj = b_scr[pl.ds(c * ch + sub * (j + 1) - 1, 1), :]
            qj = q_c * jnp.exp2(jnp.where(pc >= sub * (j + 1), b_c - rj, -jnp.inf))
            q_blocks.append(qj.astype(BF16))
            k_blocks.append(jnp.where(sid == j, kh_c, 0.0).astype(BF16))
        a_off = _dot_nt(jnp.concatenate(q_blocks, axis=-1), jnp.concatenate(k_blocks, axis=-1))
        o_c = jnp.dot(a_off.astype(BF16), vc, preferred_element_type=F32)
        o_c = o_c + _dot_nt((q_c * jnp.exp2(b_c)).astype(BF16), st.astype(BF16)) + diag[sl]
        r_last = b_scr[pl.ds((c + 1) * ch - 1, 1), :]
        ke = (kh_c * jnp.exp2(r_last - r_sub[sl])).astype(BF16)
        st = jnp.exp2(r_last) * st + _dot_tn(vc, ke)
        outs.append(o_c)
    st_scr[...] = st
    o = jnp.concatenate(outs, axis=0)

    y = o * lax.rsqrt(jnp.mean(o * o, axis=-1, keepdims=True) + EPS) * g_ref[...]
    og = og_ref[...].astype(F32)
    o_ref[...] = (y * (og * jax.nn.sigmoid(og))).astype(o_ref.dtype)


def _hgrn(proj, r_lower, r_norm_g, batch, seq, layer):
    tt = min(HGRN_TILE, seq)
    nt = seq // tt
    t = batch * seq
    nl = r_lower.shape[0]

    nh = HGRN_HEADS_PER_STEP
    hw = nh * LANES
    assert R_HEADS % nh == 0 and all(cb % nh == 0 for cb in (CB_RQ, CB_RF, CB_RI, CB_ROG))

    def col(cb):
        return pl.BlockSpec((tt, hw), lambda b, h, c: (b * nt + c, cb // nh + h))

    return pl.pallas_call(
        functools.partial(_hgrn_kernel, layer=layer),
        grid=(batch, R_HEADS // nh, nt),
        in_specs=[col(CB_RQ), col(CB_RF), col(CB_RI), col(CB_ROG),
                  pl.BlockSpec((nl, hw), lambda b, h, c: (0, h)),
                  pl.BlockSpec((1, LANES), lambda b, h, c: (0, 0))],
        out_specs=pl.BlockSpec((tt, hw), lambda b, h, c: (b * nt + c, h)),
        out_shape=jax.ShapeDtypeStruct((t, R_WIDTH), BF16),
        scratch_shapes=[pltpu.VMEM((nh, LANES, LANES), F32), pltpu.VMEM((nh, tt, LANES), F32)],
        compiler_params=_cparams(("arbitrary", "arbitrary", "arbitrary")),
        name="hgrn",
    )(proj, proj, proj, proj, r_lower, r_norm_g)


def _merge_kernel(att_ref, or_ref, ga0_ref, ga1_ref, gr0_ref, gr1_ref, x_ref, gt_ref,
                  wua_ref, wur_ref, wo_ref, g2_ref, sc_ref, sh_ref, wr_ref, br_ref,
                  x1_ref, h2_ref, rt_ref, cnt_ref, carry_scr):
    ya = jnp.dot(att_ref[...], wua_ref[...], preferred_element_type=F32)
    yr = jnp.dot(or_ref[...], wur_ref[...], preferred_element_type=F32)
    ga = jnp.concatenate([ga0_ref[...], ga1_ref[...]], axis=-1).astype(F32)
    gr = jnp.concatenate([gr0_ref[...], gr1_ref[...]], axis=-1).astype(F32)
    merged = jax.nn.sigmoid(ga) * ya + jax.nn.sigmoid(gr) * yr
    out = jnp.dot(merged.astype(BF16), wo_ref[...], preferred_element_type=F32)
    x1 = x_ref[...] + gt_ref[0] * out
    x1_ref[...] = x1
    ms = jnp.mean(x1 * x1, axis=-1, keepdims=True)
    h2 = (x1 * lax.rsqrt(ms + EPS) * g2_ref[...]) * (1.0 + sc_ref[0]) + sh_ref[0]
    h2_ref[...] = h2
    h2_top = _bf16_part(h2)
    h2_hi = h2_top.astype(BF16)
    h2_lo = (h2 - h2_top).astype(BF16)
    wr = wr_ref[...]
    p_hi = jnp.dot(h2_hi, wr, preferred_element_type=F32)
    p_lo = jnp.dot(h2_lo, wr[:, :LANES], preferred_element_type=F32)
    logits = p_hi[:, :LANES] + p_hi[:, LANES:] + p_lo + br_ref[...]
    _route_tile(logits, rt_ref, cnt_ref, carry_scr)


def _merge(att, o_r, proj, x2, gt1, w_up_a, w_up_r, w_out, g2, sc2, sh2, w_router, b_router, seq):
    t, d = x2.shape
    tm = 256
    bpt = seq // tm
    half = d // 2
    cb = half // LANES

    def gspec(cb0, k):
        return pl.BlockSpec((tm, half), lambda i: (i, cb0 // cb + k))

    def const(shape):
        return pl.BlockSpec(shape, lambda i: tuple(0 for _ in shape), pipeline_mode=pl.Buffered(1))

    def per_batch():
        return pl.BlockSpec((1, 1, d), lambda i: (i // bpt, 0, 0))

    return pl.pallas_call(
        _merge_kernel,
        grid=(t // tm,),
        in_specs=[pl.BlockSpec((tm, A_WIDTH), lambda i: (i, 0)),
                  pl.BlockSpec((tm, R_WIDTH), lambda i: (i, 0)),
                  gspec(CB_GA, 0), gspec(CB_GA, 1), gspec(CB_GR, 0), gspec(CB_GR, 1),
                  pl.BlockSpec((tm, d), lambda i: (i, 0)),
                  per_batch(),
                  const((A_WIDTH, d)), const((R_WIDTH, d)), const((d, d)),
                  const((1, d)), per_batch(), per_batch(),
                  const((d, 2 * LANES)), const((1, LANES))],
        out_specs=[pl.BlockSpec((tm, d), lambda i: (i, 0)),
                   pl.BlockSpec((tm, d), lambda i: (i, 0)),
                   pl.BlockSpec((tm, LANES), lambda i: (i, 0)),
                   pl.BlockSpec((8, LANES), lambda i: (0, 0))],
        out_shape=[jax.ShapeDtypeStruct((t, d), F32),
                   jax.ShapeDtypeStruct((t, d), F32),
                   jax.ShapeDtypeStruct((t, LANES), F32),
                   jax.ShapeDtypeStruct((8, LANES), F32)],
        scratch_shapes=[pltpu.VMEM((8, LANES), F32)],
        compiler_params=_cparams(("arbitrary",)),
        name="merge",
    )(att, o_r, proj, proj, proj, proj, x2, gt1, w_up_a, w_up_r, w_out, g2, sc2, sh2, w_router, b_router)


RT_E0, RT_E1, RT_W0, RT_W1, RT_R0, RT_R1 = 0, 1, 2, 3, 4, 5


def _route_tile(x, rt_ref, cnt_ref, carry_scr):
    tr = x.shape[0]

    @pl.when(pl.program_id(0) == 0)
    def _():
        carry_scr[...] = jnp.zeros_like(carry_scr)

    lane = lax.broadcasted_iota(jnp.int32, (tr, LANES), 1)
    ninf = -jnp.inf

    def lane_max(val):
        return jnp.max(val, axis=-1, keepdims=True)

    def first_lane(mask):
        return jnp.min(jnp.where(mask, lane, LANES), axis=-1, keepdims=True)

    is_g = lane < N_GROUPS
    gmax = lane_max(jnp.where(is_g, x, ninf))
    grp = first_lane(is_g & (x == gmax))
    eg = jnp.where(is_g, jnp.exp(x - gmax), 0.0)
    pg_top = 1.0 / jnp.sum(eg, axis=-1, keepdims=True)

    lo = N_GROUPS + grp * EXPERTS_PER_GROUP
    is_e = (lane >= lo) & (lane < lo + EXPERTS_PER_GROUP)
    emax = lane_max(jnp.where(is_e, x, ninf))
    ee = jnp.where(is_e, jnp.exp(x - emax), 0.0)
    pe = ee / jnp.sum(ee, axis=-1, keepdims=True)
    pe = jnp.where(is_e, pe, ninf)
    p0 = lane_max(pe)
    l0 = first_lane(pe == p0)
    pe1 = jnp.where(lane == l0, ninf, pe)
    p1 = lane_max(pe1)
    l1 = first_lane(pe1 == p1)
    den = p0 + p1
    w0 = pg_top * p0 / den
    w1 = pg_top * p1 / den
    e0 = l0 - N_GROUPS
    e1 = l1 - N_GROUPS

    onehot = ((lane == e0) | (lane == e1)).astype(BF16)
    r = lax.broadcasted_iota(jnp.int32, (tr, tr), 0)
    c = lax.broadcasted_iota(jnp.int32, (tr, tr), 1)
    tri = jnp.where(c < r, 1.0, 0.0).astype(BF16)
    prefix = jnp.dot(tri, onehot, preferred_element_type=F32) + carry_scr[0:1, :]
    rank0 = jnp.sum(jnp.where(lane == e0, prefix, 0.0), axis=-1, keepdims=True)
    rank1 = jnp.sum(jnp.where(lane == e1, prefix, 0.0), axis=-1, keepdims=True)
    total = carry_scr[0:1, :] + jnp.sum(onehot.astype(F32), axis=0, keepdims=True)
    carry_scr[...] = jnp.broadcast_to(total, carry_scr.shape)
    cnt_ref[...] = jnp.broadcast_to(total, cnt_ref.shape)

    rec = jnp.zeros((tr, LANES), F32)
    for k, val in ((RT_E0, e0.astype(F32)), (RT_E1, e1.astype(F32)), (RT_W0, w0), (RT_W1, w1),
                   (RT_R0, rank0), (RT_R1, rank1)):
        rec = jnp.where(lane == k, val, rec)
    rt_ref[...] = rec


def _row_copy(src_hbm, dst_buf, src_row, dst_row, sem):
    return pltpu.make_async_copy(src_hbm.at[pl.ds(src_row, 1), :], dst_buf.at[pl.ds(dst_row, 1), :], sem)


def _wait_rows(src_hbm, dst_buf, n_rows, sem):
    def body(r, _):
        _row_copy(src_hbm, dst_buf, 0, r, sem).wait()
        return 0

    lax.fori_loop(0, n_rows, body, 0, unroll=8)


def _moe_kernel(tok_ref, be_ref, nu_ref, run_ref, nxt_ref, h_hbm, w1_hbm, w3_hbm, w2_hbm, y_ref, *scratch):
    depth = GATHER_LOOKAHEAD
    bufs = scratch[:depth + 1]
    sem, w1_f, w3_f, w2_f, wsem, w1_bf, w3_bf, w2_bf = scratch[depth + 1:]
    rows = MOE_BLOCK
    i = pl.program_id(0)
    nused = nu_ref[0]

    def issue(block, parity):
        base = block * rows
        for r in range(rows):
            _row_copy(h_hbm, bufs[parity], tok_ref[base + r], r, sem.at[parity]).start()

    for first in range(depth):
        @pl.when((i == 0) & (first < nused))
        def _(first=first):
            issue(first, first)

    def weight_copies(expert, slot):
        return [pltpu.make_async_copy(src.at[expert], dst.at[slot], wsem.at[slot])
                for src, dst in ((w1_hbm, w1_f), (w3_hbm, w3_f), (w2_hbm, w2_f))]

    new_expert = (i == 0) | (be_ref[i] != be_ref[jnp.maximum(i - 1, 0)])
    wslot = run_ref[i] % 2

    @pl.when((i == 0) & (nused > 0))
    def _():
        for cp in weight_copies(be_ref[0], 0):
            cp.start()

    @pl.when((i < nused) & new_expert)
    def _():
        for cp in weight_copies(be_ref[i], wslot):
            cp.wait()

        @pl.when(nxt_ref[i] >= 0)
        def _():
            for cp in weight_copies(nxt_ref[i], 1 - wslot):
                cp.start()

        w1_bf[...] = w1_f[wslot].astype(BF16)
        w3_bf[...] = w3_f[wslot].astype(BF16)
        w2_bf[...] = w2_f[wslot].astype(BF16)

    def block_step(parity, prefetch):
        _wait_rows(h_hbm, bufs[parity], rows, sem.at[parity])
        if prefetch:
            issue(i + depth, (parity + depth) % len(bufs))
        xb = bufs[parity][...].astype(BF16)
        h1 = jnp.dot(xb, w1_bf[...], preferred_element_type=F32)
        h3 = jnp.dot(xb, w3_bf[...], preferred_element_type=F32)
        hid = (h1 * jax.nn.sigmoid(h1)) * h3
        y_ref[...] = jnp.dot(hid.astype(BF16), w2_bf[...], preferred_element_type=F32)

    for parity in range(len(bufs)):
        for prefetch in (True, False):
            more = (i + depth < nused) if prefetch else ((i < nused) & (i + depth >= nused))
            pl.when((i % len(bufs) == parity) & more)(functools.partial(block_step, parity, prefetch))

    @pl.when(i >= nused)
    def _():
        y_ref[...] = jnp.zeros_like(y_ref)


def _moe(row_tok, blk_e, nused, run_id, next_e, h2, w1, w3, w2):
    t, d = h2.shape
    r = row_tok.shape[0]
    nblk = r // MOE_BLOCK
    grid_spec = pltpu.PrefetchScalarGridSpec(
        num_scalar_prefetch=5,
        grid=(nblk,),
        in_specs=[pl.BlockSpec(memory_space=pl.ANY)] * 4,
        out_specs=pl.BlockSpec((MOE_BLOCK, d), lambda i, *_: (i, 0)),
        scratch_shapes=[pltpu.VMEM((MOE_BLOCK, d), F32)] * (GATHER_LOOKAHEAD + 1) + [
                        pltpu.SemaphoreType.DMA((GATHER_LOOKAHEAD + 1,)),
                        pltpu.VMEM((2, d, D_EXPERT), F32), pltpu.VMEM((2, d, D_EXPERT), F32),
                        pltpu.VMEM((2, D_EXPERT, d), F32), pltpu.SemaphoreType.DMA((2,)),
                        pltpu.VMEM((d, D_EXPERT), BF16), pltpu.VMEM((d, D_EXPERT), BF16),
                        pltpu.VMEM((D_EXPERT, d), BF16)],
    )
    return pl.pallas_call(
        _moe_kernel,
        grid_spec=grid_spec,
        out_shape=jax.ShapeDtypeStruct((r, d), F32),
        compiler_params=_cparams(("arbitrary",)),
        name="moe",
    )(row_tok, blk_e, nused, run_id, next_e, h2, w1, w3, w2)


def _final_kernel(dest_ref, y_hbm, x1_ref, rt_ref, gt_ref, g_ref, o_ref, *scratch, last_layer):
    depth = GATHER_LOOKAHEAD
    bufs, sem = scratch[:depth + 1], scratch[depth + 1]
    tf = x1_ref.shape[0]
    i = pl.program_id(0)
    n = pl.num_programs(0)

    def issue(step, parity):
        base = step * tf * TOP_K_INNER
        for r in range(tf):
            for k in range(TOP_K_INNER):
                _row_copy(y_hbm, bufs[parity].at[k], dest_ref[base + r * TOP_K_INNER + k], r,
                          sem.at[parity]).start()

    for first in range(depth):
        @pl.when((i == 0) & (first < n))
        def _(first=first):
            issue(first, first)

    def tile_step(parity, prefetch):
        buf = bufs[parity]
        for k in range(TOP_K_INNER):
            _wait_rows(y_hbm, buf.at[k], tf, sem.at[parity])
        if prefetch:
            issue(i + depth, (parity + depth) % len(bufs))
        rt = rt_ref[...]
        w0 = rt[:, RT_W0:RT_W0 + 1]
        w1 = rt[:, RT_W1:RT_W1 + 1]
        y = buf[0] * w0 + buf[1] * w1
        x2 = x1_ref[...] + gt_ref[0] * y
        if last_layer:
            ms = jnp.mean(x2 * x2, axis=-1, keepdims=True)
            x2 = x2 * lax.rsqrt(ms + EPS) * g_ref[...]
        o_ref[...] = x2

    for parity in range(len(bufs)):
        for prefetch in (True, False):
            more = (i + depth < n) if prefetch else (i + depth >= n)
            pl.when((i % len(bufs) == parity) & more)(functools.partial(tile_step, parity, prefetch))


def _final(dest, yr, x1, rt, gt2, final_g, seq, last_layer):
    t, d = x1.shape
    tf = 256
    bpt = seq // tf
    grid_spec = pltpu.PrefetchScalarGridSpec(
        num_scalar_prefetch=1,
        grid=(t // tf,),
        in_specs=[pl.BlockSpec(memory_space=pl.ANY),
                  pl.BlockSpec((tf, d), lambda i, dst: (i, 0)),
                  pl.BlockSpec((tf, LANES), lambda i, dst: (i, 0)),
                  pl.BlockSpec((1, 1, d), lambda i, dst: (i // bpt, 0, 0)),
                  pl.BlockSpec((1, d), lambda i, dst: (0, 0))],
        out_specs=pl.BlockSpec((tf, d), lambda i, dst: (i, 0)),
        scratch_shapes=[pltpu.VMEM((TOP_K_INNER, tf, d), F32)] * (GATHER_LOOKAHEAD + 1) + [
                        pltpu.SemaphoreType.DMA((GATHER_LOOKAHEAD + 1,))],
    )
    return pl.pallas_call(
        functools.partial(_final_kernel, last_layer=last_layer),
        grid_spec=grid_spec,
        out_shape=jax.ShapeDtypeStruct((t, d), F32),
        compiler_params=_cparams(("arbitrary",)),
        name="final",
    )(dest, yr, x1, rt, gt2, final_g)


def _dispatch_tables(rt, counts_row):
    t = rt.shape[0]
    tk = t * TOP_K_INNER
    eid = rt[:, RT_E0:RT_E1 + 1].astype(jnp.int32)
    rank = rt[:, RT_R0:RT_R1 + 1].astype(jnp.int32)
    counts = counts_row[:N_EXPERTS].astype(jnp.int32)
    padded = (counts + MOE_BLOCK - 1) // MOE_BLOCK * MOE_BLOCK
    pad_end = jnp.cumsum(padded)
    pad_start = pad_end - padded
    onehot = (eid[..., None] == jnp.arange(N_EXPERTS, dtype=jnp.int32)).astype(F32)
    base = jnp.einsum('tke,e->tk', onehot, pad_start.astype(F32), precision=HIGHEST)
    dest = (base.astype(jnp.int32) + rank).reshape(tk)
    r = tk + N_EXPERTS * MOE_BLOCK
    tok = jnp.repeat(jnp.arange(t, dtype=jnp.int32), TOP_K_INNER)
    row_tok = jnp.zeros((r,), jnp.int32).at[dest].set(tok)
    nblk = r // MOE_BLOCK
    blk_start = jnp.arange(nblk, dtype=jnp.int32) * MOE_BLOCK
    blk_e = jnp.minimum(jnp.sum(pad_end[None, :] <= blk_start[:, None], axis=1), N_EXPERTS - 1).astype(jnp.int32)
    nused = (pad_end[-1:] // MOE_BLOCK).astype(jnp.int32)
    idx = jnp.arange(nblk, dtype=jnp.int32)
    is_new = jnp.concatenate([jnp.ones((1,), bool), blk_e[1:] != blk_e[:-1]])
    run_id = (jnp.cumsum(is_new.astype(jnp.int32)) - 1).astype(jnp.int32)
    later = (idx[None, :] > idx[:, None]) & (blk_e[None, :] != blk_e[:, None]) & (idx[None, :] < nused[0])
    first_later = jnp.min(jnp.where(later, idx[None, :], nblk), axis=1)
    next_e = jnp.where(first_later < nblk, blk_e[jnp.minimum(first_later, nblk - 1)], -1).astype(jnp.int32)
    return dest, row_tok, blk_e, nused, run_id, next_e


def kernel(x, c, norm1_g, norm2_g, final_g, w_ada, b_ada, w_in, r_lower, r_norm_g,
           w_up_a, w_up_r, w_out, w_rg, b_rg, w_re, b_re, w1, w3, w2):
    batch, seq, d = x.shape
    t = batch * seq
    depth = w_in.shape[0]
    x2 = x.reshape(t, d)
    c_pad = jnp.zeros((-(-batch // 16) * 16, d), F32).at[:batch].set(c)
    for l in range(depth):
        mod = _ada(c_pad, w_ada[l], b_ada[l][None, :])[:batch]
        sh1, sc1, gt1, sh2, sc2, gt2 = [m[:, None, :] for m in jnp.split(mod, 6, axis=-1)]
        proj = _proj(x2, norm1_g[l][None, :], sc1, sh1, w_in[l], seq)
        att = _moba(proj, batch, seq)
        o_r = _hgrn(proj, r_lower, r_norm_g[l][None, :], batch, seq, l)
        w_router = jnp.zeros((d, LANES), F32).at[:, :N_GROUPS].set(w_rg[l]) \
            .at[:, N_GROUPS:N_GROUPS + N_EXPERTS].set(w_re[l])
        b_router = jnp.zeros((1, LANES), F32).at[0, :N_GROUPS].set(b_rg[l]) \
            .at[0, N_GROUPS:N_GROUPS + N_EXPERTS].set(b_re[l])
        w_router_top = _bf16_part(w_router)
        w_router = jnp.concatenate([w_router_top, w_router - w_router_top], axis=-1).astype(BF16)
        x1, h2, rt, cnt = _merge(att, o_r, proj, x2, gt1, w_up_a[l].astype(BF16), w_up_r[l].astype(BF16),
                                 w_out[l].astype(BF16), norm2_g[l][None, :], sc2, sh2, w_router, b_router, seq)
        dest, row_tok, blk_e, nused, run_id, next_e = _dispatch_tables(rt, cnt[0])
        yr = _moe(row_tok, blk_e, nused, run_id, next_e, h2, w1[l], w3[l], w2[l])
        x2 = _final(dest, yr, x1, rt, gt2, final_g[None, :], seq, l == depth - 1)
    return x2.reshape(batch, seq, d)
```
